```python
import math
import jax, jax.numpy as jnp
from jax import lax
import numpy as np

D_MODEL = 2048
BATCH = 4
SEQ = 4096
DEPTH = 4

CONV_CH = D_MODEL // 2
CONV_K = 3
N_HEADS = 8
HEAD_DIM = 128
ATT_WIDTH = N_HEADS * HEAD_DIM
MOBA_BLOCK = 256
MOBA_TOPK = 3
Q_CHUNK = 32
NUM_BUCKETS = 32
MAX_DISTANCE = 128
D_FF = -(-8 * D_MODEL // (3 * 256)) * 256
EPS = 1e-6
IN_COLS = 3 * CONV_CH + 3 * ATT_WIDTH + 2 * D_MODEL
SPLIT_POINTS = (CONV_CH, 2 * CONV_CH, 3 * CONV_CH,
                3 * CONV_CH + ATT_WIDTH, 3 * CONV_CH + 2 * ATT_WIDTH,
                3 * CONV_CH + 3 * ATT_WIDTH, 3 * CONV_CH + 3 * ATT_WIDTH + D_MODEL)

kernel_name = 'hybrid_shortconv_moba_block'


def rms_norm(x, g):
    xf = x.astype(jnp.float32)
    y = xf * lax.rsqrt(jnp.mean(xf * xf, axis=-1, keepdims=True) + EPS)
    return (y * g.astype(jnp.float32)).astype(x.dtype)


def causal_short_conv(u, w):
    s = u.shape[1]
    up = jnp.pad(u, ((0, 0), (CONV_K - 1, 0), (0, 0)))
    y = w[0] * up[:, 0:s]
    for j in range(1, CONV_K):
        y = y + w[j] * up[:, j:j + s]
    return y


def t5_bucket(dist):
    n = jnp.maximum(dist, 0)
    max_exact = NUM_BUCKETS // 2
    nf = jnp.maximum(n, 1).astype(jnp.float32)
    large = max_exact + (jnp.log(nf / max_exact) / math.log(MAX_DISTANCE / max_exact)
                         * (NUM_BUCKETS - max_exact)).astype(jnp.int32)
    large = jnp.minimum(large, NUM_BUCKETS - 1)
    return jnp.where(n < max_exact, n, large)


def moba_attention(q, k, v, rel_bias):
    b, s, h, dh = q.shape
    nb = max(-(-s // MOBA_BLOCK), MOBA_TOPK)
    sp = nb * MOBA_BLOCK
    q = q.transpose(0, 2, 1, 3)
    k = jnp.pad(k.transpose(0, 2, 1, 3), ((0, 0), (0, 0), (0, sp - s), (0, 0)))
    v = jnp.pad(v.transpose(0, 2, 1, 3), ((0, 0), (0, 0), (0, sp - s), (0, 0)))
    kblk = k.reshape(b, h, nb, MOBA_BLOCK, dh)
    vblk = v.reshape(b, h, nb, MOBA_BLOCK, dh)
    kmean = jnp.mean(kblk.astype(jnp.float32), axis=3)
    table_t = rel_bias.astype(jnp.float32).T
    scale = dh ** -0.5
    b_idx = jnp.arange(b)[:, None, None, None]
    h_idx = jnp.arange(h)[None, :, None, None]
    h_idx5 = jnp.arange(h)[None, :, None, None, None]
    offs = jnp.arange(MOBA_BLOCK)
    blk_ids = jnp.arange(nb)
    sel_slot = jnp.arange(MOBA_TOPK)

    def chunk(c):
        start = c * Q_CHUNK
        blk = start // MOBA_BLOCK
        qc = lax.dynamic_slice_in_dim(q, start, Q_CHUNK, axis=2)
        qpos = start + jnp.arange(Q_CHUNK)
        k_own = lax.dynamic_slice_in_dim(k, blk * MOBA_BLOCK, MOBA_BLOCK, axis=2)
        v_own = lax.dynamic_slice_in_dim(v, blk * MOBA_BLOCK, MOBA_BLOCK, axis=2)
        dist_own = qpos[:, None] - (blk * MOBA_BLOCK + offs)[None, :]
        l_own = (jnp.einsum('bhqd,bhkd->bhqk', qc, k_own, preferred_element_type=jnp.float32) * scale
                 + table_t[:, t5_bucket(dist_own)][None])
        l_own = jnp.where(dist_own >= 0, l_own, -jnp.inf)
        gate = jnp.einsum('bhqd,bhnd->bhqn', qc.astype(jnp.float32), kmean)
        gate = jnp.where(blk_ids < blk, gate, -jnp.inf)
        _, sel = lax.top_k(gate, MOBA_TOPK)
        valid = sel_slot < blk
        k_sel = kblk[b_idx, h_idx, sel]
        v_sel = vblk[b_idx, h_idx, sel]
        dist_sel = qpos[None, None, :, None, None] - (sel[..., None] * MOBA_BLOCK + offs)
        l_sel = (jnp.einsum('bhqd,bhqjkd->bhqjk', qc, k_sel, preferred_element_type=jnp.float32) * scale
                 + table_t[h_idx5, t5_bucket(dist_sel)])
        l_sel = jnp.where(valid[:, None], l_sel, -jnp.inf)
        logits = jnp.concatenate(
            [l_own, l_sel.reshape(b, h, Q_CHUNK, MOBA_TOPK * MOBA_BLOCK)], axis=-1)
        p = jax.nn.softmax(logits, axis=-1).astype(v.dtype)
        p_own = p[..., :MOBA_BLOCK]
        p_sel = p[..., MOBA_BLOCK:].reshape(b, h, Q_CHUNK, MOBA_TOPK, MOBA_BLOCK)
        return (jnp.einsum('bhqk,bhkd->bhqd', p_own, v_own)
                + jnp.einsum('bhqjk,bhqjkd->bhqd', p_sel, v_sel))

    out = lax.map(chunk, jnp.arange(s // Q_CHUNK))
    return out.transpose(1, 0, 3, 2, 4).reshape(b, s, h, dh)


def hybrid_mixer(xn, w_in, conv_w, w_conv_out, w_attn_out, w_mix_out, rel_bias):
    b, s, _ = xn.shape
    proj = xn @ w_in
    h_in, g_b, g_c, q, k, v, gate_conv, gate_att = jnp.split(proj, SPLIT_POINTS, axis=-1)
    y_conv = (g_b * causal_short_conv(g_c * h_in, conv_w)) @ w_conv_out
    att = moba_attention(q.reshape(b, s, N_HEADS, HEAD_DIM),
                         k.reshape(b, s, N_HEADS, HEAD_DIM),
                         v.reshape(b, s, N_HEADS, HEAD_DIM), rel_bias)
    y_att = att.reshape(b, s, ATT_WIDTH) @ w_attn_out
    merged = jax.nn.sigmoid(gate_conv) * y_conv + jax.nn.sigmoid(gate_att) * y_att
    return merged @ w_mix_out


def swiglu(xn, w_gate, w_up, w_down):
    return (jax.nn.silu(xn @ w_gate) * (xn @ w_up)) @ w_down


def setup_inputs(seed: int = 0) -> dict:
    key = jax.random.key(seed)
    ks = jax.random.split(key, 14)

    def normal(k, shape, scale):
        return jax.random.normal(k, shape, jnp.float32) * scale

    return {
        'x': normal(ks[0], (BATCH, SEQ, D_MODEL), 1.0),
        'w_in': normal(ks[1], (DEPTH, D_MODEL, IN_COLS), D_MODEL ** -0.5),
        'conv_w': normal(ks[2], (DEPTH, CONV_K, CONV_CH), CONV_K ** -0.5),
        'w_conv_out': normal(ks[3], (DEPTH, CONV_CH, D_MODEL), CONV_CH ** -0.5),
        'w_attn_out': normal(ks[4], (DEPTH, ATT_WIDTH, D_MODEL), ATT_WIDTH ** -0.5),
        'w_mix_out': normal(ks[5], (DEPTH, D_MODEL, D_MODEL), D_MODEL ** -0.5),
        'rel_bias': normal(ks[6], (NUM_BUCKETS, N_HEADS), 0.5),
        'norm_mix': 1.0 + normal(ks[7], (DEPTH, D_MODEL), 0.02),
        'norm_ffn': 1.0 + normal(ks[8], (DEPTH, D_MODEL), 0.02),
        'w_ffn_gate': normal(ks[9], (DEPTH, D_MODEL, D_FF), D_MODEL ** -0.5),
        'w_ffn_up': normal(ks[10], (DEPTH, D_MODEL, D_FF), D_MODEL ** -0.5),
        'w_ffn_down': normal(ks[11], (DEPTH, D_FF, D_MODEL), D_FF ** -0.5),
        'norm_final': 1.0 + normal(ks[12], (D_MODEL,), 0.02),
    }


def reference(x, w_in, conv_w, w_conv_out, w_attn_out, w_mix_out, rel_bias,
              norm_mix, norm_ffn, w_ffn_gate, w_ffn_up, w_ffn_down, norm_final):
    for l in range(DEPTH):
        x = x + hybrid_mixer(rms_norm(x, norm_mix[l]), w_in[l], conv_w[l],
                             w_conv_out[l], w_attn_out[l], w_mix_out[l], rel_bias)
        x = x + swiglu(rms_norm(x, norm_ffn[l]), w_ffn_gate[l], w_ffn_up[l], w_ffn_down[l])
    return rms_norm(x, norm_final)
```

```python
import functools
import math

import numpy as np
import jax
import jax.numpy as jnp
from jax import lax
from jax.experimental import pallas as pl
from jax.experimental.pallas import tpu as pltpu

CONV_K = 3
N_HEADS = 8
HEAD_DIM = 128
MOBA_BLOCK = 256
MOBA_TOPK = 3
NUM_BUCKETS = 32
MAX_DISTANCE = 128
EPS = 1e-6

_BF16 = jnp.bfloat16
_F32 = jnp.float32
_V7X_VMEM_LIMIT_BYTES = 56 * 1024 * 1024


def _params(semantics):
    return pltpu.CompilerParams(dimension_semantics=semantics,
                                vmem_limit_bytes=_V7X_VMEM_LIMIT_BYTES)


def _rms_scale(x):
    return lax.rsqrt(jnp.mean(x * x, axis=-1, keepdims=True) + EPS)


def _norm_kernel(x_ref, g_ref, o_ref):
    x = x_ref[...]
    o_ref[...] = ((x * _rms_scale(x)) * g_ref[...]).astype(o_ref.dtype)


def _rmsnorm(x, g, tm=512):
    t, d = x.shape
    return pl.pallas_call(
        _norm_kernel,
        grid=(t // tm,),
        in_specs=[pl.BlockSpec((tm, d), lambda i: (i, 0)),
                  pl.BlockSpec((1, d), lambda i: (0, 0))],
        out_specs=pl.BlockSpec((tm, d), lambda i: (i, 0)),
        out_shape=jax.ShapeDtypeStruct((t, d), _BF16),
        compiler_params=_params(("parallel",)),
        name="rmsnorm",
    )(x, g)


def _matmul_kernel(x_ref, w_ref, o_ref):
    o_ref[...] = jnp.dot(x_ref[...], w_ref[...],
                         preferred_element_type=_F32).astype(o_ref.dtype)


def _in_proj(xn, w_in, layer, tm=1024, tn=1024):
    t, d = xn.shape
    n = w_in.shape[2]
    return pl.pallas_call(
        _matmul_kernel,
        grid=(t // tm, n // tn),
        in_specs=[pl.BlockSpec((tm, d), lambda i, j: (i, 0)),
                  pl.BlockSpec((None, d, tn), lambda i, j: (layer, 0, j))],
        out_specs=pl.BlockSpec((tm, tn), lambda i, j: (i, j)),
        out_shape=jax.ShapeDtypeStruct((t, n), _BF16),
        compiler_params=_params(("parallel", "parallel")),
        name="in_proj",
    )(xn, w_in)


def _t5_bucket_np(dist):
    n = np.maximum(dist, 0)
    max_exact = NUM_BUCKETS // 2
    nf = np.maximum(n, 1).astype(np.float64)
    large = max_exact + (np.log(nf / max_exact) / math.log(MAX_DISTANCE / max_exact)
                         * (NUM_BUCKETS - max_exact)).astype(np.int32)
    large = np.minimum(large, NUM_BUCKETS - 1)
    return np.where(n < max_exact, n, large).astype(np.int32)


def _bias_tables(rel_bias):
    qi = np.arange(MOBA_BLOCK)[:, None]
    ki = np.arange(MOBA_BLOCK)[None, :]
    table_t = rel_bias.astype(_F32).T
    own = table_t[:, _t5_bucket_np(qi - ki)]
    own = jnp.where(jnp.asarray(qi >= ki)[None], own, -jnp.inf)
    prev = table_t[:, _t5_bucket_np(qi + MOBA_BLOCK - ki)]
    return own, prev


def _attn_kernel(q_ref, k_ref, v_ref, own_ref, prev_ref, far_ref, o_ref, kmean_ref, *, nb):
    h = pl.program_id(1)
    i = pl.program_id(2)
    blk = MOBA_BLOCK
    scale = HEAD_DIM ** -0.5
    nt = (((1,), (1,)), ((), ()))

    @pl.when(i == 0)
    def _():
        kf = k_ref[...].astype(_F32).reshape(nb, blk, HEAD_DIM)
        kmean_ref[...] = jnp.sum(kf, axis=1) * (1.0 / blk)

    q = q_ref[...]

    gate = lax.dot_general(kmean_ref[...], q.astype(_F32), nt,
                           precision=lax.Precision.HIGHEST,
                           preferred_element_type=_F32)
    blk_id = lax.broadcasted_iota(jnp.int32, (nb, blk), 0)
    past = blk_id < i
    gate = jnp.where(past, gate, -jnp.inf)
    rank = jnp.zeros((nb, blk), jnp.int32)
    for n in range(nb):
        row = gate[n:n + 1, :]
        beats = (row > gate) | ((row == gate) & (n < blk_id))
        rank = rank + beats.astype(jnp.int32)
    sel = ((rank < MOBA_TOPK) & past).astype(_F32)
    sel = jnp.concatenate([sel, jnp.zeros((128 - nb, blk), _F32)], axis=0)
    sel_q = sel.T
    lane = lax.broadcasted_iota(jnp.int32, (blk, 128), 1)

    def selected(j):
        return jnp.sum(jnp.where(lane == j, sel_q, 0.0), axis=1, keepdims=True) > 0.0

    def block(j):
        start = pl.multiple_of(j * blk, blk)
        return k_ref[pl.ds(start, blk), :], v_ref[pl.ds(start, blk), :]

    def scores(kj):
        return lax.dot_general(q, kj, nt, preferred_element_type=_F32) * scale

    def update(carry, s, vj):
        m, l, acc = carry
        m_new = jnp.maximum(m, jnp.max(s, axis=1, keepdims=True))
        alpha = jnp.exp(m - m_new)
        p = jnp.exp(s - m_new)
        l = alpha * l + jnp.sum(p, axis=1, keepdims=True)
        acc = alpha * acc + jnp.dot(p.astype(vj.dtype), vj, preferred_element_type=_F32)
        return m_new, l, acc

    k_own, v_own = block(i)
    s = scores(k_own) + own_ref[...]
    m = jnp.max(s, axis=1, keepdims=True)
    p = jnp.exp(s - m)
    carry = (m, jnp.sum(p, axis=1, keepdims=True),
             jnp.dot(p.astype(v_own.dtype), v_own, preferred_element_type=_F32))

    jp = jnp.maximum(i - 1, 0)
    k_prev, v_prev = block(jp)
    s = jnp.where(selected(i - 1), scores(k_prev) + prev_ref[...], -jnp.inf)
    carry = update(carry, s, v_prev)

    far_bias = far_ref[h]

    def far_body(j, carry):
        kj, vj = block(j)
        s = jnp.where(selected(j), scores(kj) + far_bias, -jnp.inf)
        return update(carry, s, vj)

    m, l, acc = lax.fori_loop(0, jnp.maximum(i - 1, 0), far_body, carry)
    o_ref[...] = (acc / l).astype(o_ref.dtype)


def _moba_attention(proj, bias_own, bias_prev, bias_far, batch, seq, q_col, k_col, v_col):
    t = proj.shape[0]
    nb = seq // MOBA_BLOCK
    blk = MOBA_BLOCK
    kernel = functools.partial(_attn_kernel, nb=nb)
    return pl.pallas_call(
        kernel,
        grid=(batch, N_HEADS, nb),
        in_specs=[
            pl.BlockSpec((blk, HEAD_DIM), lambda b, h, i: (b * nb + i, q_col + h)),
            pl.BlockSpec((seq, HEAD_DIM), lambda b, h, i: (b, k_col + h)),
            pl.BlockSpec((seq, HEAD_DIM), lambda b, h, i: (b, v_col + h)),
            pl.BlockSpec((None, blk, blk), lambda b, h, i: (h, 0, 0)),
            pl.BlockSpec((None, blk, blk), lambda b, h, i: (h, 0, 0)),
            pl.BlockSpec(memory_space=pltpu.SMEM),
        ],
        out_specs=pl.BlockSpec((blk, HEAD_DIM), lambda b, h, i: (b * nb + i, h)),
        out_shape=jax.ShapeDtypeStruct((t, N_HEADS * HEAD_DIM), _BF16),
        scratch_shapes=[pltpu.VMEM((nb, HEAD_DIM), _F32)],
        compiler_params=_params(("parallel", "parallel", "arbitrary")),
        name="moba_attention",
    )(proj, proj, proj, bias_own, bias_prev, bias_far)


def _mixer_kernel(hin_ref, gb_ref, gc_ref, hin_halo_ref, gc_halo_ref, gconv_ref, gatt_ref,
                  att_ref, x_ref, cw_ref, wco_ref, wao_ref, wmix_ref, g_ref,
                  h_ref, hn_ref, *, tiles_per_seq):
    i = pl.program_id(0)
    tm = hin_ref.shape[0]
    halo = hin_halo_ref.shape[0]

    u = gc_ref[...].astype(_F32) * hin_ref[...].astype(_F32)
    keep = (i % tiles_per_seq != 0).astype(_F32)
    uh = gc_halo_ref[...].astype(_F32) * hin_halo_ref[...].astype(_F32) * keep
    prev1 = uh[halo - 1:halo, :]
    prev2 = uh[halo - 2:halo - 1, :]
    row = lax.broadcasted_iota(jnp.int32, u.shape, 0)
    u1 = jnp.where(row == 0, prev1, pltpu.roll(u, 1, 0))
    u2 = jnp.where(row == 0, prev2, jnp.where(row == 1, prev1, pltpu.roll(u, 2, 0)))
    cw = cw_ref[...]
    y = cw[0:1, :] * u2 + cw[1:2, :] * u1 + cw[2:3, :] * u
    yc = (gb_ref[...].astype(_F32) * y).astype(_BF16)

    y_conv = jnp.dot(yc, wco_ref[...], preferred_element_type=_F32)
    y_att = jnp.dot(att_ref[...], wao_ref[...], preferred_element_type=_F32)
    merged = (jax.nn.sigmoid(gconv_ref[...].astype(_F32)) * y_conv
              + jax.nn.sigmoid(gatt_ref[...].astype(_F32)) * y_att)
    h = x_ref[...] + jnp.dot(merged.astype(_BF16), wmix_ref[...], preferred_element_type=_F32)
    h_ref[...] = h
    hn_ref[...] = ((h * _rms_scale(h)) * g_ref[...]).astype(hn_ref.dtype)


def _mixer_out(proj, att, x, conv_w, w_co, w_ao, w_mix, g_ffn, layer, seq, tm=256, halo=16):
    t, d = x.shape
    c = conv_w.shape[2]
    aw = att.shape[1]
    assert seq % tm == 0 and tm % halo == 0
    gate_col = (3 * c + 3 * aw) // d
    resident = pl.Buffered(1)
    kernel = functools.partial(_mixer_kernel, tiles_per_seq=seq // tm)
    halo_idx = lambda i: jnp.maximum(i * (tm // halo) - 1, 0)
    return pl.pallas_call(
        kernel,
        grid=(t // tm,),
        in_specs=[
            pl.BlockSpec((tm, c), lambda i: (i, 0)),
            pl.BlockSpec((tm, c), lambda i: (i, 1)),
            pl.BlockSpec((tm, c), lambda i: (i, 2)),
            pl.BlockSpec((halo, c), lambda i: (halo_idx(i), 0)),
            pl.BlockSpec((halo, c), lambda i: (halo_idx(i), 2)),
            pl.BlockSpec((tm, d), lambda i: (i, gate_col)),
            pl.BlockSpec((tm, d), lambda i: (i, gate_col + 1)),
            pl.BlockSpec((tm, aw), lambda i: (i, 0)),
            pl.BlockSpec((tm, d), lambda i: (i, 0)),
            pl.BlockSpec((None, CONV_K, c), lambda i: (layer, 0, 0)),
            pl.BlockSpec((None, c, d), lambda i: (layer, 0, 0), pipeline_mode=resident),
            pl.BlockSpec((None, aw, d), lambda i: (layer, 0, 0), pipeline_mode=resident),
            pl.BlockSpec((None, d, d), lambda i: (layer, 0, 0), pipeline_mode=resident),
            pl.BlockSpec((1, d), lambda i: (0, 0)),
        ],
        out_specs=[pl.BlockSpec((tm, d), lambda i: (i, 0)),
                   pl.BlockSpec((tm, d), lambda i: (i, 0))],
        out_shape=[jax.ShapeDtypeStruct((t, d), _F32),
                   jax.ShapeDtypeStruct((t, d), _BF16)],
        compiler_params=_params(("parallel",)),
        name="mixer_out",
    )(proj, proj, proj, proj, proj, proj, proj, att, x, conv_w, w_co, w_ao, w_mix, g_ffn)


def _ffn_kernel(hn_ref, h_ref, wg_ref, wu_ref, wd_ref, g_ref, *refs, emit_x):
    if emit_x:
        x_ref, xn_ref, acc_ref = refs
    else:
        xn_ref, acc_ref = refs
    f = pl.program_id(1)

    @pl.when(f == 0)
    def _():
        acc_ref[...] = jnp.zeros_like(acc_ref)

    hn = hn_ref[...]
    a = jnp.dot(hn, wg_ref[...], preferred_element_type=_F32)
    b = jnp.dot(hn, wu_ref[...], preferred_element_type=_F32)
    act = (a * jax.nn.sigmoid(a)) * b
    acc_ref[...] += jnp.dot(act.astype(_BF16), wd_ref[...], preferred_element_type=_F32)

    @pl.when(f == pl.num_programs(1) - 1)
    def _():
        x = h_ref[...] + acc_ref[...]
        if emit_x:
            x_ref[...] = x
        xn_ref[...] = ((x * _rms_scale(x)) * g_ref[...]).astype(xn_ref.dtype)


def _ffn(hn, h, w_gate, w_up, w_down, g_next, layer, emit_x, xn_dtype, tm=512, tf=512):
    t, d = h.shape
    ff = w_gate.shape[2]
    kernel = functools.partial(_ffn_kernel, emit_x=emit_x)
    row_spec = pl.BlockSpec((tm, d), lambda i, f: (i, 0))
    out_specs = [row_spec, row_spec] if emit_x else [row_spec]
    out_shape = [jax.ShapeDtypeStruct((t, d), xn_dtype)]
    if emit_x:
        out_shape = [jax.ShapeDtypeStruct((t, d), _F32)] + out_shape
    return pl.pallas_call(
        kernel,
        grid=(t // tm, ff // tf),
        in_specs=[
            row_spec,
            row_spec,
            pl.BlockSpec((None, d, tf), lambda i, f: (layer, 0, f)),
            pl.BlockSpec((None, d, tf), lambda i, f: (layer, 0, f)),
            pl.BlockSpec((None, tf, d), lambda i, f: (layer, f, 0)),
            pl.BlockSpec((1, d), lambda i, f: (0, 0)),
        ],
        out_specs=out_specs,
        out_shape=out_shape,
        scratch_shapes=[pltpu.VMEM((tm, d), _F32)],
        compiler_params=_params(("parallel", "arbitrary")),
        name="ffn",
    )(hn, h, w_gate, w_up, w_down, g_next)


def kernel(x, w_in, conv_w, w_conv_out, w_attn_out, w_mix_out, rel_bias, norm_mix, norm_ffn,
           w_ffn_gate, w_ffn_up, w_ffn_down, norm_final):
    batch, seq, d = x.shape
    depth = w_in.shape[0]
    c = conv_w.shape[2]
    aw = N_HEADS * HEAD_DIM
    t = batch * seq
    assert seq % MOBA_BLOCK == 0 and seq // MOBA_BLOCK >= MOBA_TOPK
    assert w_in.shape[2] == 3 * c + 3 * aw + 2 * d

    w_in, w_co, w_ao, w_mix = (w.astype(_BF16) for w in (w_in, w_conv_out, w_attn_out, w_mix_out))
    w_gate, w_up, w_down = (w.astype(_BF16) for w in (w_ffn_gate, w_ffn_up, w_ffn_down))
    bias_own, bias_prev = _bias_tables(rel_bias)
    bias_far = rel_bias.astype(_F32)[NUM_BUCKETS - 1]
    q_col = 3 * c // HEAD_DIM
    k_col = q_col + N_HEADS
    v_col = k_col + N_HEADS

    x = x.reshape(t, d)
    xn = _rmsnorm(x, norm_mix[0:1])
    for layer in range(depth):
        proj = _in_proj(xn, w_in, layer)
        att = _moba_attention(proj, bias_own, bias_prev, bias_far, batch, seq, q_col, k_col, v_col)
        h, hn = _mixer_out(proj, att, x, conv_w, w_co, w_ao, w_mix, norm_ffn[layer:layer + 1],
                           layer, seq)
        if layer + 1 < depth:
            x, xn = _ffn(hn, h, w_gate, w_up, w_down, norm_mix[layer + 1:layer + 2], layer,
                         emit_x=True, xn_dtype=_BF16)
        else:
            (out,) = _ffn(hn, h, w_gate, w_up, w_down, norm_final.reshape(1, d), layer,
                          emit_x=False, xn_dtype=_F32)
    return out.reshape(batch, seq, d)
```

```python
import functools
import math

import numpy as np
import jax
import jax.numpy as jnp
from jax import lax
from jax.experimental import pallas as pl
from jax.experimental.pallas import tpu as pltpu

CONV_K = 3
N_HEADS = 8
HEAD_DIM = 128
MOBA_BLOCK = 256
MOBA_TOPK = 3
NUM_BUCKETS = 32
MAX_DISTANCE = 128
EPS = 1e-6

_BF16 = jnp.bfloat16
_F32 = jnp.float32
_V7X_VMEM_LIMIT_BYTES = 56 * 1024 * 1024


def _params(semantics):
    return pltpu.CompilerParams(dimension_semantics=semantics,
                                vmem_limit_bytes=_V7X_VMEM_LIMIT_BYTES)


def _rms_scale(x):
    return lax.rsqrt(jnp.mean(x * x, axis=-1, keepdims=True) + EPS)


def _norm_kernel(x_ref, g_ref, o_ref):
    x = x_ref[...]
    o_ref[...] = ((x * _rms_scale(x)) * g_ref[...]).astype(o_ref.dtype)


def _rmsnorm(x, g, tm=512):
    t, d = x.shape
    return pl.pallas_call(
        _norm_kernel,
        grid=(t // tm,),
        in_specs=[pl.BlockSpec((tm, d), lambda i: (i, 0)),
                  pl.BlockSpec((1, d), lambda i: (0, 0))],
        out_specs=pl.BlockSpec((tm, d), lambda i: (i, 0)),
        out_shape=jax.ShapeDtypeStruct((t, d), _BF16),
        compiler_params=_params(("parallel",)),
        name="rmsnorm",
    )(x, g)


def _matmul_kernel(x_ref, w_ref, o_ref):
    o_ref[...] = jnp.dot(x_ref[...], w_ref[...],
                         preferred_element_type=_F32).astype(o_ref.dtype)


def _in_proj(xn, w_in, layer, tm=1024, tn=1024):
    t, d = xn.shape
    n = w_in.shape[2]
    return pl.pallas_call(
        _matmul_kernel,
        grid=(t // tm, n // tn),
        in_specs=[pl.BlockSpec((tm, d), lambda i, j: (i, 0)),
                  pl.BlockSpec((None, d, tn), lambda i, j: (layer, 0, j))],
        out_specs=pl.BlockSpec((tm, tn), lambda i, j: (i, j)),
        out_shape=jax.ShapeDtypeStruct((t, n), _BF16),
        compiler_params=_params(("parallel", "parallel")),
        name="in_proj",
    )(xn, w_in)


def _t5_bucket_np(dist):
    n = np.maximum(dist, 0)
    max_exact = NUM_BUCKETS // 2
    nf = np.maximum(n, 1).astype(np.float64)
    large = max_exact + (np.log(nf / max_exact) / math.log(MAX_DISTANCE / max_exact)
                         * (NUM_BUCKETS - max_exact)).astype(np.int32)
    large = np.minimum(large, NUM_BUCKETS - 1)
    return np.where(n < max_exact, n, large).astype(np.int32)


def _bias_tables(rel_bias):
    blk = MOBA_BLOCK
    width = 2 * blk
    e = np.arange(width)
    d_own, d_prev = e - (blk - 1), e + 1
    idx = np.stack([_t5_bucket_np(d_own), _t5_bucket_np(d_prev)])
    table_t = rel_bias.astype(_F32).T
    heads = table_t.shape[0]
    vals = jnp.take(table_t, jnp.asarray(idx), axis=1)
    vals = vals - table_t[:, NUM_BUCKETS - 1][:, None, None]
    causal = np.stack([d_own >= 0, np.ones(width, bool)])
    vals = jnp.where(jnp.asarray(causal)[None], vals, -jnp.inf)
    skew = jnp.broadcast_to(vals[:, :, None, :], (heads, 2, blk, width))
    skew = skew.reshape(heads, 2, blk * width)[:, :, :blk * (width - 1)]
    skew = skew.reshape(heads, 2, blk, width - 1)[:, :, :, blk - 1:width - 1]
    return skew[:, 0], skew[:, 1]


def _attn_kernel(q_ref, k_ref, v_ref, own_ref, prev_ref, o_ref, kmean_ref, vt_ref, s_ref, p_ref,
                 *, nb, heads):
    i = pl.program_id(2)
    blk = MOBA_BLOCK
    dh = HEAD_DIM
    scale = dh ** -0.5

    @pl.when(i == 0)
    def _():
        for hh in range(heads):
            cols = slice(hh * dh, (hh + 1) * dh)
            kf = k_ref[:, cols].astype(_F32).reshape(nb, blk, dh)
            kmean_ref[hh] = jnp.sum(kf, axis=1) * (1.0 / blk)
            for n in range(nb):
                rows = slice(n * blk, (n + 1) * blk)
                vt_ref[hh, :, rows] = v_ref[rows, cols].T

    blk_id = lax.broadcasted_iota(jnp.int32, (nb, blk), 0)
    past = blk_id < i
    q_t, masks = [], []
    for hh in range(heads):
        qt = q_ref[:, hh * dh:(hh + 1) * dh].T
        q_t.append(qt)
        gate = jnp.dot(kmean_ref[hh], qt.astype(_F32), precision=lax.Precision.HIGHEST,
                       preferred_element_type=_F32)
        gate = jnp.where(past, gate, -jnp.inf)
        rank = jnp.zeros((nb, blk), jnp.int32)
        for n in range(nb):
            row = gate[n:n + 1, :]
            beats = (row > gate) | ((row == gate) & (n < blk_id))
            rank = rank + beats.astype(jnp.int32)
        masks.append(jnp.where((rank < MOBA_TOPK) & past, 0.0, -jnp.inf))

    def attend(hh, n_blocks):
        cols = slice(hh * dh, (hh + 1) * dh)
        keys = n_blocks * blk
        s_all = jnp.dot(k_ref[0:keys, cols], q_t[hh], preferred_element_type=_F32)
        m = None
        for j in range(n_blocks):
            rows = slice(j * blk, (j + 1) * blk)
            if j == n_blocks - 1:
                bias = own_ref[hh]
            elif j == n_blocks - 2:
                bias = prev_ref[hh] + masks[hh][j:j + 1, :]
            else:
                bias = masks[hh][j:j + 1, :]
            s = s_all[rows, :] * scale + bias
            s_ref[hh, rows, :] = s
            mj = jnp.max(s, axis=0, keepdims=True)
            m = mj if m is None else jnp.maximum(m, mj)
        l = jnp.zeros((1, blk), _F32)
        for j in range(n_blocks):
            rows = slice(j * blk, (j + 1) * blk)
            p = jnp.exp(s_ref[hh, rows, :] - m)
            l = l + jnp.sum(p, axis=0, keepdims=True)
            p_ref[hh, rows, :] = p.astype(p_ref.dtype)
        acc = jnp.dot(vt_ref[hh, :, 0:keys], p_ref[hh, 0:keys, :], preferred_element_type=_F32)
        o_ref[:, cols] = (acc * (1.0 / l)).T.astype(o_ref.dtype)

    for n in range(nb):
        @pl.when(i == n)
        def _(n=n):
            for hh in range(heads):
                attend(hh, n + 1)


def _moba_attention(proj, bias_own, bias_prev, batch, seq, q_col, heads=2):
    t = proj.shape[0]
    nb = seq // MOBA_BLOCK
    blk = MOBA_BLOCK
    w = heads * HEAD_DIM
    groups = N_HEADS // heads
    q_blk = q_col // w
    kernel = functools.partial(_attn_kernel, nb=nb, heads=heads)
    return pl.pallas_call(
        kernel,
        grid=(batch, groups, nb),
        in_specs=[
            pl.BlockSpec((blk, w), lambda b, g, i: (b * nb + i, q_blk + g)),
            pl.BlockSpec((seq, w), lambda b, g, i: (b, q_blk + groups + g)),
            pl.BlockSpec((seq, w), lambda b, g, i: (b, q_blk + 2 * groups + g)),
            pl.BlockSpec((heads, blk, blk), lambda b, g, i: (g, 0, 0)),
            pl.BlockSpec((heads, blk, blk), lambda b, g, i: (g, 0, 0)),
        ],
        out_specs=pl.BlockSpec((blk, w), lambda b, g, i: (b * nb + i, g)),
        out_shape=jax.ShapeDtypeStruct((t, N_HEADS * HEAD_DIM), _BF16),
        scratch_shapes=[pltpu.VMEM((heads, nb, HEAD_DIM), _F32),
                        pltpu.VMEM((heads, HEAD_DIM, seq), _BF16),
                        pltpu.VMEM((heads, seq, blk), _F32),
                        pltpu.VMEM((heads, seq, blk), _BF16)],
        compiler_params=_params(("parallel", "parallel", "arbitrary")),
        name="moba_attention",
    )(proj, proj, proj, bias_own, bias_prev)


def _mixer_kernel(hin_ref, gb_ref, gc_ref, hin_halo_ref, gc_halo_ref, gconv_ref, gatt_ref,
                  att_ref, x_ref, cw_ref, wco_ref, wao_ref, wmix_ref, g_ref,
                  h_ref, hn_ref, *, tiles_per_seq):
    i = pl.program_id(0)
    tm = hin_ref.shape[0]
    halo = hin_halo_ref.shape[0]

    u = gc_ref[...].astype(_F32) * hin_ref[...].astype(_F32)
    keep = (i % tiles_per_seq != 0).astype(_F32)
    uh = gc_halo_ref[...].astype(_F32) * hin_halo_ref[...].astype(_F32) * keep
    prev1 = uh[halo - 1:halo, :]
    prev2 = uh[halo - 2:halo - 1, :]
    row = lax.broadcasted_iota(jnp.int32, u.shape, 0)
    u1 = jnp.where(row == 0, prev1, pltpu.roll(u, 1, 0))
    u2 = jnp.where(row == 0, prev2, jnp.where(row == 1, prev1, pltpu.roll(u, 2, 0)))
    cw = cw_ref[...]
    y = cw[0:1, :] * u2 + cw[1:2, :] * u1 + cw[2:3, :] * u
    yc = (gb_ref[...].astype(_F32) * y).astype(_BF16)

    y_conv = jnp.dot(yc, wco_ref[...], preferred_element_type=_F32)
    y_att = jnp.dot(att_ref[...], wao_ref[...], preferred_element_type=_F32)
    merged = (jax.nn.sigmoid(gconv_ref[...].astype(_F32)) * y_conv
              + jax.nn.sigmoid(gatt_ref[...].astype(_F32)) * y_att)
    h = x_ref[...] + jnp.dot(merged.astype(_BF16), wmix_ref[...], preferred_element_type=_F32)
    h_ref[...] = h
    hn_ref[...] = ((h * _rms_scale(h)) * g_ref[...]).astype(hn_ref.dtype)


def _mixer_out(proj, att, x, conv_w, w_co, w_ao, w_mix, g_ffn, layer, seq, tm=256, halo=16):
    t, d = x.shape
    c = conv_w.shape[2]
    aw = att.shape[1]
    assert seq % tm == 0 and tm % halo == 0
    gate_col = (3 * c + 3 * aw) // d
    resident = pl.Buffered(1)
    kernel = functools.partial(_mixer_kernel, tiles_per_seq=seq // tm)
    halo_idx = lambda i: jnp.maximum(i * (tm // halo) - 1, 0)
    return pl.pallas_call(
        kernel,
        grid=(t // tm,),
        in_specs=[
            pl.BlockSpec((tm, c), lambda i: (i, 0)),
            pl.BlockSpec((tm, c), lambda i: (i, 1)),
            pl.BlockSpec((tm, c), lambda i: (i, 2)),
            pl.BlockSpec((halo, c), lambda i: (halo_idx(i), 0)),
            pl.BlockSpec((halo, c), lambda i: (halo_idx(i), 2)),
            pl.BlockSpec((tm, d), lambda i: (i, gate_col)),
            pl.BlockSpec((tm, d), lambda i: (i, gate_col + 1)),
            pl.BlockSpec((tm, aw), lambda i: (i, 0)),
            pl.BlockSpec((tm, d), lambda i: (i, 0)),
            pl.BlockSpec((None, CONV_K, c), lambda i: (layer, 0, 0)),
            pl.BlockSpec((None, c, d), lambda i: (layer, 0, 0), pipeline_mode=resident),
            pl.BlockSpec((None, aw, d), lambda i: (layer, 0, 0), pipeline_mode=resident),
            pl.BlockSpec((None, d, d), lambda i: (layer, 0, 0), pipeline_mode=resident),
            pl.BlockSpec((1, d), lambda i: (0, 0)),
        ],
        out_specs=[pl.BlockSpec((tm, d), lambda i: (i, 0)),
                   pl.BlockSpec((tm, d), lambda i: (i, 0))],
        out_shape=[jax.ShapeDtypeStruct((t, d), _F32),
                   jax.ShapeDtypeStruct((t, d), _BF16)],
        compiler_params=_params(("parallel",)),
        name="mixer_out",
    )(proj, proj, proj, proj, proj, proj, proj, att, x, conv_w, w_co, w_ao, w_mix, g_ffn)


def _ffn_kernel(hn_ref, h_ref, wg_ref, wu_ref, wd_ref, g_ref, *refs, emit_x):
    if emit_x:
        x_ref, xn_ref, acc_ref = refs
    else:
        xn_ref, acc_ref = refs
    f = pl.program_id(1)

    @pl.when(f == 0)
    def _():
        acc_ref[...] = jnp.zeros_like(acc_ref)

    hn = hn_ref[...]
    a = jnp.dot(hn, wg_ref[...], preferred_element_type=_F32)
    b = jnp.dot(hn, wu_ref[...], preferred_element_type=_F32)
    act = (a * jax.nn.sigmoid(a)) * b
    acc_ref[...] += jnp.dot(act.astype(_BF16), wd_ref[...], preferred_element_type=_F32)

    @pl.when(f == pl.num_programs(1) - 1)
    def _():
        x = h_ref[...] + acc_ref[...]
        if emit_x:
            x_ref[...] = x
        xn_ref[...] = ((x * _rms_scale(x)) * g_ref[...]).astype(xn_ref.dtype)


def _ffn(hn, h, w_gate, w_up, w_down, g_next, layer, emit_x, xn_dtype, tm=512, tf=512):
    t, d = h.shape
    ff = w_gate.shape[2]
    kernel = functools.partial(_ffn_kernel, emit_x=emit_x)
    row_spec = pl.BlockSpec((tm, d), lambda i, f: (i, 0))
    out_specs = [row_spec, row_spec] if emit_x else [row_spec]
    out_shape = [jax.ShapeDtypeStruct((t, d), xn_dtype)]
    if emit_x:
        out_shape = [jax.ShapeDtypeStruct((t, d), _F32)] + out_shape
    return pl.pallas_call(
        kernel,
        grid=(t // tm, ff // tf),
        in_specs=[
            row_spec,
            row_spec,
            pl.BlockSpec((None, d, tf), lambda i, f: (layer, 0, f)),
            pl.BlockSpec((None, d, tf), lambda i, f: (layer, 0, f)),
            pl.BlockSpec((None, tf, d), lambda i, f: (layer, f, 0)),
            pl.BlockSpec((1, d), lambda i, f: (0, 0)),
        ],
        out_specs=out_specs,
        out_shape=out_shape,
        scratch_shapes=[pltpu.VMEM((tm, d), _F32)],
        compiler_params=_params(("parallel", "arbitrary")),
        name="ffn",
    )(hn, h, w_gate, w_up, w_down, g_next)


def kernel(x, w_in, conv_w, w_conv_out, w_attn_out, w_mix_out, rel_bias, norm_mix, norm_ffn,
           w_ffn_gate, w_ffn_up, w_ffn_down, norm_final):
    batch, seq, d = x.shape
    depth = w_in.shape[0]
    c = conv_w.shape[2]
    aw = N_HEADS * HEAD_DIM
    t = batch * seq
    assert seq % MOBA_BLOCK == 0 and seq // MOBA_BLOCK >= MOBA_TOPK
    assert w_in.shape[2] == 3 * c + 3 * aw + 2 * d

    w_in, w_co, w_ao, w_mix = (w.astype(_BF16) for w in (w_in, w_conv_out, w_attn_out, w_mix_out))
    w_gate, w_up, w_down = (w.astype(_BF16) for w in (w_ffn_gate, w_ffn_up, w_ffn_down))
    bias_own, bias_prev = _bias_tables(rel_bias)
    q_col = 3 * c

    x = x.reshape(t, d)
    xn = _rmsnorm(x, norm_mix[0:1])
    for layer in range(depth):
        proj = _in_proj(xn, w_in, layer)
        att = _moba_attention(proj, bias_own, bias_prev, batch, seq, q_col)
        h, hn = _mixer_out(proj, att, x, conv_w, w_co, w_ao, w_mix, norm_ffn[layer:layer + 1],
                           layer, seq)
        if layer + 1 < depth:
            x, xn = _ffn(hn, h, w_gate, w_up, w_down, norm_mix[layer + 1:layer + 2], layer,
                         emit_x=True, xn_dtype=_BF16)
        else:
            (out,) = _ffn(hn, h, w_gate, w_up, w_down, norm_final.reshape(1, d), layer,
                          emit_x=False, xn_dtype=_F32)
    return out.reshape(batch, seq, d)
```

```python
import functools
import math

import numpy as np
import jax
import jax.numpy as jnp
from jax import lax
from jax.experimental import pallas as pl
from jax.experimental.pallas import tpu as pltpu

CONV_K = 3
N_HEADS = 8
HEAD_DIM = 128
MOBA_BLOCK = 256
MOBA_TOPK = 3
NUM_BUCKETS = 32
MAX_DISTANCE = 128
EPS = 1e-6
_LOG2_E = math.log2(math.e)

_BF16 = jnp.bfloat16
_F32 = jnp.float32
_V7X_VMEM_LIMIT_BYTES = 56 * 1024 * 1024


def _params(semantics):
    return pltpu.CompilerParams(dimension_semantics=semantics,
                                vmem_limit_bytes=_V7X_VMEM_LIMIT_BYTES)


def _rms_scale(x):
    return lax.rsqrt(jnp.mean(x * x, axis=-1, keepdims=True) + EPS)


def _norm_kernel(x_ref, g_ref, o_ref):
    x = x_ref[...]
    o_ref[...] = ((x * _rms_scale(x)) * g_ref[...]).astype(o_ref.dtype)


def _rmsnorm(x, g, tm=512):
    t, d = x.shape
    return pl.pallas_call(
        _norm_kernel,
        grid=(t // tm,),
        in_specs=[pl.BlockSpec((tm, d), lambda i: (i, 0)),
                  pl.BlockSpec((1, d), lambda i: (0, 0))],
        out_specs=pl.BlockSpec((tm, d), lambda i: (i, 0)),
        out_shape=jax.ShapeDtypeStruct((t, d), _BF16),
        compiler_params=_params(("parallel",)),
        name="rmsnorm",
    )(x, g)


def _matmul_kernel(x_ref, w_ref, o_ref):
    o_ref[...] = jnp.dot(x_ref[...], w_ref[...],
                         preferred_element_type=_F32).astype(o_ref.dtype)


def _in_proj(xn, w_in, layer, tm=1024, tn=1024):
    t, d = xn.shape
    n = w_in.shape[2]
    return pl.pallas_call(
        _matmul_kernel,
        grid=(t // tm, n // tn),
        in_specs=[pl.BlockSpec((tm, d), lambda i, j: (i, 0)),
                  pl.BlockSpec((None, d, tn), lambda i, j: (layer, 0, j))],
        out_specs=pl.BlockSpec((tm, tn), lambda i, j: (i, j)),
        out_shape=jax.ShapeDtypeStruct((t, n), _BF16),
        compiler_params=_params(("parallel", "parallel")),
        name="in_proj",
    )(xn, w_in)


def _t5_bucket_np(dist):
    n = np.maximum(dist, 0)
    max_exact = NUM_BUCKETS // 2
    nf = np.maximum(n, 1).astype(np.float64)
    large = max_exact + (np.log(nf / max_exact) / math.log(MAX_DISTANCE / max_exact)
                         * (NUM_BUCKETS - max_exact)).astype(np.int32)
    large = np.minimum(large, NUM_BUCKETS - 1)
    return np.where(n < max_exact, n, large).astype(np.int32)


def _bias_tables(rel_bias):
    blk = MOBA_BLOCK
    width = 2 * blk
    e = np.arange(width)
    d_own, d_prev = e - (blk - 1), e + 1
    idx = np.stack([_t5_bucket_np(d_own), _t5_bucket_np(d_prev)])
    table_t = rel_bias.astype(_F32).T
    heads = table_t.shape[0]
    vals = jnp.take(table_t, jnp.asarray(idx), axis=1)
    vals = vals - table_t[:, NUM_BUCKETS - 1][:, None, None]
    causal = np.stack([d_own >= 0, np.ones(width, bool)])
    vals = jnp.where(jnp.asarray(causal)[None], vals, -jnp.inf)
    skew = jnp.broadcast_to(vals[:, :, None, :], (heads, 2, blk, width))
    skew = skew.reshape(heads, 2, blk * width)[:, :, :blk * (width - 1)]
    skew = skew.reshape(heads, 2, blk, width - 1)[:, :, :, blk - 1:width - 1]
    skew = skew * _LOG2_E
    return skew[:, 0], skew[:, 1]


def _attn_kernel(q_ref, k_ref, v_ref, own_ref, prev_ref, o_ref,
                 kmean_ref, vt_ref, qt_ref, mask_ref, m_ref, acc_ref, *, nb, heads):
    i = pl.program_id(1)
    blk = MOBA_BLOCK
    dh = HEAD_DIM
    log2_scale = dh ** -0.5 * _LOG2_E

    @pl.when(i == 0)
    def _():
        @pl.loop(0, nb)
        def _(n):
            rows = pl.ds(pl.multiple_of(n * blk, blk), blk)
            for h in range(heads):
                cols = slice(h * dh, (h + 1) * dh)
                kf = k_ref[rows, cols].astype(_F32)
                kmean_ref[h, pl.ds(n, 1), :] = jnp.sum(kf, axis=0, keepdims=True) * (1.0 / blk)
                vt_ref[h, n, 0:dh, :] = v_ref[rows, cols].T
                vt_ref[h, n, dh:, :] = jnp.ones((vt_ref.shape[2] - dh, blk), vt_ref.dtype)

    blk_id = lax.broadcasted_iota(jnp.int32, (nb, blk), 0)
    past = blk_id < i
    for h in range(heads):
        qt = q_ref[:, h * dh:(h + 1) * dh].T
        qt_ref[h] = qt
        gate = jnp.dot(kmean_ref[h], qt.astype(_F32), precision=lax.Precision.HIGHEST,
                       preferred_element_type=_F32)
        gate = jnp.where(past, gate, -jnp.inf)
        rank = jnp.zeros((nb, blk), jnp.int32)
        for n in range(nb):
            row = gate[n:n + 1, :]
            beats = (row > gate) | ((row == gate) & (n < blk_id))
            rank = rank + beats.astype(jnp.int32)
        mask_ref[h] = jnp.where((rank < MOBA_TOPK) & past, 0.0, -jnp.inf)

    def process(j, bias_of_head, first):
        start = pl.multiple_of(j * blk, blk)

        def qk(h):
            return jnp.dot(k_ref[pl.ds(start, blk), h * dh:(h + 1) * dh], qt_ref[h],
                           preferred_element_type=_F32)

        ahead = heads // 2
        raw = [qk(h) for h in range(ahead)]
        for h in range(heads):
            if h + ahead < heads:
                raw.append(qk(h + ahead))
            s = raw[h] * log2_scale + bias_of_head(h)
            m_new = jnp.max(s, axis=0, keepdims=True)
            if not first:
                m_old = m_ref[h]
                m_new = jnp.maximum(m_old, m_new)
            p = jnp.exp2(s - m_new).astype(vt_ref.dtype)
            pv = jnp.dot(vt_ref[h, j], p, preferred_element_type=_F32)
            if first:
                acc_ref[h] = pv
            else:
                acc_ref[h] = jnp.exp2(m_old - m_new) * acc_ref[h] + pv
            m_ref[h] = m_new

    process(i, lambda h: own_ref[h], first=True)
    jp = jnp.maximum(i - 1, 0)
    process(jp, lambda h: prev_ref[h] + mask_ref[h, pl.ds(jp, 1), :], first=False)

    @pl.loop(0, jp)
    def _(j):
        process(j, lambda h: mask_ref[h, pl.ds(j, 1), :], first=False)

    for h in range(heads):
        acc = acc_ref[h]
        out = acc[0:dh, :] * (1.0 / acc[dh:dh + 1, :])
        o_ref[:, h * dh:(h + 1) * dh] = out.T.astype(o_ref.dtype)


def _moba_attention(proj, bias_own, bias_prev, batch, seq, q_col):
    t = proj.shape[0]
    nb = seq // MOBA_BLOCK
    blk = MOBA_BLOCK
    heads = N_HEADS
    w = heads * HEAD_DIM
    q_blk = q_col // w
    ones_rows = 16
    once = pl.Buffered(1)
    kernel = functools.partial(_attn_kernel, nb=nb, heads=heads)
    return pl.pallas_call(
        kernel,
        grid=(batch, nb),
        in_specs=[
            pl.BlockSpec((blk, w), lambda b, i: (b * nb + i, q_blk)),
            pl.BlockSpec((seq, w), lambda b, i: (b, q_blk + 1), pipeline_mode=once),
            pl.BlockSpec((seq, w), lambda b, i: (b, q_blk + 2), pipeline_mode=once),
            pl.BlockSpec((heads, blk, blk), lambda b, i: (0, 0, 0), pipeline_mode=once),
            pl.BlockSpec((heads, blk, blk), lambda b, i: (0, 0, 0), pipeline_mode=once),
        ],
        out_specs=pl.BlockSpec((blk, w), lambda b, i: (b * nb + i, 0)),
        out_shape=jax.ShapeDtypeStruct((t, w), _BF16),
        scratch_shapes=[
            pltpu.VMEM((heads, nb, HEAD_DIM), _F32),
            pltpu.VMEM((heads, nb, HEAD_DIM + ones_rows, blk), _BF16),
            pltpu.VMEM((heads, HEAD_DIM, blk), _BF16),
            pltpu.VMEM((heads, nb, blk), _F32),
            pltpu.VMEM((heads, 1, blk), _F32),
            pltpu.VMEM((heads, HEAD_DIM + ones_rows, blk), _F32),
        ],
        compiler_params=_params(("parallel", "arbitrary")),
        name="moba_attention",
    )(proj, proj, proj, bias_own, bias_prev)


def _mixer_kernel(hin_ref, gb_ref, gc_ref, hin_halo_ref, gc_halo_ref, gconv_ref, gatt_ref,
                  att_ref, x_ref, cw_ref, wco_ref, wao_ref, wmix_ref, g_ref,
                  h_ref, hn_ref, *, tiles_per_seq):
    i = pl.program_id(0)
    tm = hin_ref.shape[0]
    halo = hin_halo_ref.shape[0]

    u = gc_ref[...].astype(_F32) * hin_ref[...].astype(_F32)
    keep = (i % tiles_per_seq != 0).astype(_F32)
    uh = gc_halo_ref[...].astype(_F32) * hin_halo_ref[...].astype(_F32) * keep
    prev1 = uh[halo - 1:halo, :]
    prev2 = uh[halo - 2:halo - 1, :]
    row = lax.broadcasted_iota(jnp.int32, u.shape, 0)
    u1 = jnp.where(row == 0, prev1, pltpu.roll(u, 1, 0))
    u2 = jnp.where(row == 0, prev2, jnp.where(row == 1, prev1, pltpu.roll(u, 2, 0)))
    cw = cw_ref[...]
    y = cw[0:1, :] * u2 + cw[1:2, :] * u1 + cw[2:3, :] * u
    yc = (gb_ref[...].astype(_F32) * y).astype(_BF16)

    y_conv = jnp.dot(yc, wco_ref[...], preferred_element_type=_F32)
    y_att = jnp.dot(att_ref[...], wao_ref[...], preferred_element_type=_F32)
    merged = (jax.nn.sigmoid(gconv_ref[...].astype(_F32)) * y_conv
              + jax.nn.sigmoid(gatt_ref[...].astype(_F32)) * y_att)
    h = x_ref[...] + jnp.dot(merged.astype(_BF16), wmix_ref[...], preferred_element_type=_F32)
    h_ref[...] = h
    hn_ref[...] = ((h * _rms_scale(h)) * g_ref[...]).astype(hn_ref.dtype)


def _mixer_out(proj, att, x, conv_w, w_co, w_ao, w_mix, g_ffn, layer, seq, tm=256, halo=16):
    t, d = x.shape
    c = conv_w.shape[2]
    aw = att.shape[1]
    assert seq % tm == 0 and tm % halo == 0
    gate_col = (3 * c + 3 * aw) // d
    resident = pl.Buffered(1)
    kernel = functools.partial(_mixer_kernel, tiles_per_seq=seq // tm)
    halo_idx = lambda i: jnp.maximum(i * (tm // halo) - 1, 0)
    return pl.pallas_call(
        kernel,
        grid=(t // tm,),
        in_specs=[
            pl.BlockSpec((tm, c), lambda i: (i, 0)),
            pl.BlockSpec((tm, c), lambda i: (i, 1)),
            pl.BlockSpec((tm, c), lambda i: (i, 2)),
            pl.BlockSpec((halo, c), lambda i: (halo_idx(i), 0)),
            pl.BlockSpec((halo, c), lambda i: (halo_idx(i), 2)),
            pl.BlockSpec((tm, d), lambda i: (i, gate_col)),
            pl.BlockSpec((tm, d), lambda i: (i, gate_col + 1)),
            pl.BlockSpec((tm, aw), lambda i: (i, 0)),
            pl.BlockSpec((tm, d), lambda i: (i, 0)),
            pl.BlockSpec((None, CONV_K, c), lambda i: (layer, 0, 0)),
            pl.BlockSpec((None, c, d), lambda i: (layer, 0, 0), pipeline_mode=resident),
            pl.BlockSpec((None, aw, d), lambda i: (layer, 0, 0), pipeline_mode=resident),
            pl.BlockSpec((None, d, d), lambda i: (layer, 0, 0), pipeline_mode=resident),
            pl.BlockSpec((1, d), lambda i: (0, 0)),
        ],
        out_specs=[pl.BlockSpec((tm, d), lambda i: (i, 0)),
                   pl.BlockSpec((tm, d), lambda i: (i, 0))],
        out_shape=[jax.ShapeDtypeStruct((t, d), _F32),
                   jax.ShapeDtypeStruct((t, d), _BF16)],
        compiler_params=_params(("parallel",)),
        name="mixer_out",
    )(proj, proj, proj, proj, proj, proj, proj, att, x, conv_w, w_co, w_ao, w_mix, g_ffn)


def _ffn_kernel(hn_ref, h_ref, wg_ref, wu_ref, wd_ref, g_ref, *refs, emit_x):
    if emit_x:
        x_ref, xn_ref, acc_ref = refs
    else:
        xn_ref, acc_ref = refs
    f = pl.program_id(1)

    @pl.when(f == 0)
    def _():
        acc_ref[...] = jnp.zeros_like(acc_ref)

    hn = hn_ref[...]
    a = jnp.dot(hn, wg_ref[...], preferred_element_type=_F32)
    b = jnp.dot(hn, wu_ref[...], preferred_element_type=_F32)
    act = (a * jax.nn.sigmoid(a)) * b
    acc_ref[...] += jnp.dot(act.astype(_BF16), wd_ref[...], preferred_element_type=_F32)

    @pl.when(f == pl.num_programs(1) - 1)
    def _():
        x = h_ref[...] + acc_ref[...]
        if emit_x:
            x_ref[...] = x
        xn_ref[...] = ((x * _rms_scale(x)) * g_ref[...]).astype(xn_ref.dtype)


def _ffn(hn, h, w_gate, w_up, w_down, g_next, layer, emit_x, xn_dtype, tm=512, tf=512):
    t, d = h.shape
    ff = w_gate.shape[2]
    kernel = functools.partial(_ffn_kernel, emit_x=emit_x)
    row_spec = pl.BlockSpec((tm, d), lambda i, f: (i, 0))
    out_specs = [row_spec, row_spec] if emit_x else [row_spec]
    out_shape = [jax.ShapeDtypeStruct((t, d), xn_dtype)]
    if emit_x:
        out_shape = [jax.ShapeDtypeStruct((t, d), _F32)] + out_shape
    return pl.pallas_call(
        kernel,
        grid=(t // tm, ff // tf),
        in_specs=[
            row_spec,
            row_spec,
            pl.BlockSpec((None, d, tf), lambda i, f: (layer, 0, f)),
            pl.BlockSpec((None, d, tf), lambda i, f: (layer, 0, f)),
            pl.BlockSpec((None, tf, d), lambda i, f: (layer, f, 0)),
            pl.BlockSpec((1, d), lambda i, f: (0, 0)),
        ],
        out_specs=out_specs,
        out_shape=out_shape,
        scratch_shapes=[pltpu.VMEM((tm, d), _F32)],
        compiler_params=_params(("parallel", "arbitrary")),
        name="ffn",
    )(hn, h, w_gate, w_up, w_down, g_next)


def kernel(x, w_in, conv_w, w_conv_out, w_attn_out, w_mix_out, rel_bias, norm_mix, norm_ffn,
           w_ffn_gate, w_ffn_up, w_ffn_down, norm_final):
    batch, seq, d = x.shape
    depth = w_in.shape[0]
    c = conv_w.shape[2]
    aw = N_HEADS * HEAD_DIM
    t = batch * seq
    assert seq % MOBA_BLOCK == 0 and seq // MOBA_BLOCK >= MOBA_TOPK
    assert w_in.shape[2] == 3 * c + 3 * aw + 2 * d

    w_in, w_co, w_ao, w_mix = (w.astype(_BF16) for w in (w_in, w_conv_out, w_attn_out, w_mix_out))
    w_gate, w_up, w_down = (w.astype(_BF16) for w in (w_ffn_gate, w_ffn_up, w_ffn_down))
    bias_own, bias_prev = _bias_tables(rel_bias)
    q_col = 3 * c

    x = x.reshape(t, d)
    xn = _rmsnorm(x, norm_mix[0:1])
    for layer in range(depth):
        proj = _in_proj(xn, w_in, layer)
        att = _moba_attention(proj, bias_own, bias_prev, batch, seq, q_col)
        h, hn = _mixer_out(proj, att, x, conv_w, w_co, w_ao, w_mix, norm_ffn[layer:layer + 1],
                           layer, seq)
        if layer + 1 < depth:
            x, xn = _ffn(hn, h, w_gate, w_up, w_down, norm_mix[layer + 1:layer + 2], layer,
                         emit_x=True, xn_dtype=_BF16)
        else:
            (out,) = _ffn(hn, h, w_gate, w_up, w_down, norm_final.reshape(1, d), layer,
                          emit_x=False, xn_dtype=_F32)
    return out.reshape(batch, seq, d)
```

```python
import functools
import math

import numpy as np
import jax
import jax.numpy as jnp
from jax import lax
from jax.experimental import pallas as pl
from jax.experimental.pallas import tpu as pltpu

CONV_K = 3
N_HEADS = 8
HEAD_DIM = 128
MOBA_BLOCK = 256
MOBA_TOPK = 3
NUM_BUCKETS = 32
MAX_DISTANCE = 128
EPS = 1e-6
_LOG2_E = math.log2(math.e)

_BF16 = jnp.bfloat16
_F32 = jnp.float32
_V7X_VMEM_LIMIT_BYTES = 56 * 1024 * 1024


def _params(semantics):
    return pltpu.CompilerParams(dimension_semantics=semantics,
                                vmem_limit_bytes=_V7X_VMEM_LIMIT_BYTES)


def _rms_scale(x):
    return lax.rsqrt(jnp.mean(x * x, axis=-1, keepdims=True) + EPS)


def _norm_kernel(x_ref, g_ref, o_ref):
    x = x_ref[...]
    o_ref[...] = ((x * _rms_scale(x)) * g_ref[...]).astype(o_ref.dtype)


def _rmsnorm(x, g, tm=512):
    t, d = x.shape
    return pl.pallas_call(
        _norm_kernel,
        grid=(t // tm,),
        in_specs=[pl.BlockSpec((tm, d), lambda i: (i, 0)),
                  pl.BlockSpec((1, d), lambda i: (0, 0))],
        out_specs=pl.BlockSpec((tm, d), lambda i: (i, 0)),
        out_shape=jax.ShapeDtypeStruct((t, d), _BF16),
        compiler_params=_params(("parallel",)),
        name="rmsnorm",
    )(x, g)


def _matmul_kernel(x_ref, w_ref, o_ref):
    o_ref[...] = jnp.dot(x_ref[...], w_ref[...],
                         preferred_element_type=_F32).astype(o_ref.dtype)


def _in_proj(xn, w_in, layer, tm=1024, tn=2048):
    t, d = xn.shape
    n = w_in.shape[2]
    return pl.pallas_call(
        _matmul_kernel,
        grid=(t // tm, n // tn),
        in_specs=[pl.BlockSpec((tm, d), lambda i, j: (i, 0)),
                  pl.BlockSpec((None, d, tn), lambda i, j: (layer, 0, j))],
        out_specs=pl.BlockSpec((tm, tn), lambda i, j: (i, j)),
        out_shape=jax.ShapeDtypeStruct((t, n), _BF16),
        compiler_params=_params(("parallel", "parallel")),
        name="in_proj",
    )(xn, w_in)


def _t5_bucket_np(dist):
    n = np.maximum(dist, 0)
    max_exact = NUM_BUCKETS // 2
    nf = np.maximum(n, 1).astype(np.float64)
    large = max_exact + (np.log(nf / max_exact) / math.log(MAX_DISTANCE / max_exact)
                         * (NUM_BUCKETS - max_exact)).astype(np.int32)
    large = np.minimum(large, NUM_BUCKETS - 1)
    return np.where(n < max_exact, n, large).astype(np.int32)


def _bias_tables(rel_bias):
    blk = MOBA_BLOCK
    width = 2 * blk
    e = np.arange(width)
    d_own, d_prev = e - (blk - 1), e + 1
    idx = np.stack([_t5_bucket_np(d_own), _t5_bucket_np(d_prev)])
    table_t = rel_bias.astype(_F32).T
    heads = table_t.shape[0]
    vals = jnp.take(table_t, jnp.asarray(idx), axis=1)
    vals = vals - table_t[:, NUM_BUCKETS - 1][:, None, None]
    causal = np.stack([d_own >= 0, np.ones(width, bool)])
    vals = jnp.where(jnp.asarray(causal)[None], vals, -jnp.inf)
    skew = jnp.broadcast_to(vals[:, :, None, :], (heads, 2, blk, width))
    skew = skew.reshape(heads, 2, blk * width)[:, :, :blk * (width - 1)]
    skew = skew.reshape(heads, 2, blk, width - 1)[:, :, :, blk - 1:width - 1]
    skew = skew * _LOG2_E
    return skew[:, 0], skew[:, 1]


def _attn_kernel(q_ref, k_ref, v_ref, own_ref, prev_ref, o_ref,
                 kmean_ref, vt_ref, qt_ref, mask_ref, m_ref, acc_ref, *, nb, heads):
    i = pl.program_id(1)
    blk = MOBA_BLOCK
    dh = HEAD_DIM
    log2_scale = dh ** -0.5 * _LOG2_E

    @pl.when(i == 0)
    def _():
        @pl.loop(0, nb)
        def _(n):
            rows = pl.ds(pl.multiple_of(n * blk, blk), blk)
            for h in range(heads):
                cols = slice(h * dh, (h + 1) * dh)
                kf = k_ref[rows, cols].astype(_F32)
                kmean_ref[h, pl.ds(n, 1), :] = jnp.sum(kf, axis=0, keepdims=True) * (1.0 / blk)
                vt_ref[h, n, 0:dh, :] = v_ref[rows, cols].T
                vt_ref[h, n, dh:, :] = jnp.ones((vt_ref.shape[2] - dh, blk), vt_ref.dtype)

    blk_id = lax.broadcasted_iota(jnp.int32, (nb, blk), 0)
    past = blk_id < i
    for h in range(heads):
        qt = q_ref[:, h * dh:(h + 1) * dh].T
        qt_ref[h] = qt
        gate = jnp.dot(kmean_ref[h], qt.astype(_F32), precision=lax.Precision.HIGHEST,
                       preferred_element_type=_F32)
        gate = jnp.where(past, gate, -jnp.inf)
        rank = jnp.zeros((nb, blk), jnp.int32)
        for n in range(nb):
            row = gate[n:n + 1, :]
            beats = (row > gate) | ((row == gate) & (n < blk_id))
            rank = rank + beats.astype(jnp.int32)
        mask_ref[h] = jnp.where((rank < MOBA_TOPK) & past, 0.0, -jnp.inf)

    def process(j, bias_of_head, first):
        start = pl.multiple_of(j * blk, blk)

        def qk(h):
            return jnp.dot(k_ref[pl.ds(start, blk), h * dh:(h + 1) * dh], qt_ref[h],
                           preferred_element_type=_F32)

        ahead = heads // 2
        raw = [qk(h) for h in range(ahead)]
        for h in range(heads):
            if h + ahead < heads:
                raw.append(qk(h + ahead))
            s = raw[h] * log2_scale + bias_of_head(h)
            m_new = jnp.max(s, axis=0, keepdims=True)
            if not first:
                m_old = m_ref[h]
                m_new = jnp.maximum(m_old, m_new)
            p = jnp.exp2(s - m_new).astype(vt_ref.dtype)
            pv = jnp.dot(vt_ref[h, j], p, preferred_element_type=_F32)
            if first:
                acc_ref[h] = pv
            else:
                acc_ref[h] = jnp.exp2(m_old - m_new) * acc_ref[h] + pv
            m_ref[h] = m_new

    process(i, lambda h: own_ref[h], first=True)
    jp = jnp.maximum(i - 1, 0)
    process(jp, lambda h: prev_ref[h] + mask_ref[h, pl.ds(jp, 1), :], first=False)

    @pl.loop(0, jp)
    def _(j):
        process(j, lambda h: mask_ref[h, pl.ds(j, 1), :], first=False)

    for h in range(heads):
        acc = acc_ref[h]
        out = acc[0:dh, :] * (1.0 / acc[dh:dh + 1, :])
        o_ref[:, h * dh:(h + 1) * dh] = out.T.astype(o_ref.dtype)


def _moba_attention(proj, bias_own, bias_prev, batch, seq, q_col):
    t = proj.shape[0]
    nb = seq // MOBA_BLOCK
    blk = MOBA_BLOCK
    heads = N_HEADS
    w = heads * HEAD_DIM
    q_blk = q_col // w
    ones_rows = 16
    once = pl.Buffered(1)
    kernel = functools.partial(_attn_kernel, nb=nb, heads=heads)
    return pl.pallas_call(
        kernel,
        grid=(batch, nb),
        in_specs=[
            pl.BlockSpec((blk, w), lambda b, i: (b * nb + i, q_blk)),
            pl.BlockSpec((seq, w), lambda b, i: (b, q_blk + 1), pipeline_mode=once),
            pl.BlockSpec((seq, w), lambda b, i: (b, q_blk + 2), pipeline_mode=once),
            pl.BlockSpec((heads, blk, blk), lambda b, i: (0, 0, 0), pipeline_mode=once),
            pl.BlockSpec((heads, blk, blk), lambda b, i: (0, 0, 0), pipeline_mode=once),
        ],
        out_specs=pl.BlockSpec((blk, w), lambda b, i: (b * nb + i, 0)),
        out_shape=jax.ShapeDtypeStruct((t, w), _BF16),
        scratch_shapes=[
            pltpu.VMEM((heads, nb, HEAD_DIM), _F32),
            pltpu.VMEM((heads, nb, HEAD_DIM + ones_rows, blk), _BF16),
            pltpu.VMEM((heads, HEAD_DIM, blk), _BF16),
            pltpu.VMEM((heads, nb, blk), _F32),
            pltpu.VMEM((heads, 1, blk), _F32),
            pltpu.VMEM((heads, HEAD_DIM + ones_rows, blk), _F32),
        ],
        compiler_params=_params(("parallel", "arbitrary")),
        name="moba_attention",
    )(proj, proj, proj, bias_own, bias_prev)


def _mixer_kernel(hin_ref, gb_ref, gc_ref, hin_halo_ref, gc_halo_ref, gconv_ref, gatt_ref,
                  att_ref, x_ref, cw_ref, wco_ref, wao_ref, wmix_ref, g_ref,
                  h_ref, hn_ref, *, tiles_per_seq):
    i = pl.program_id(0)
    tm = hin_ref.shape[0]
    halo = hin_halo_ref.shape[0]

    u = gc_ref[...].astype(_F32) * hin_ref[...].astype(_F32)
    keep = (i % tiles_per_seq != 0).astype(_F32)
    uh = gc_halo_ref[...].astype(_F32) * hin_halo_ref[...].astype(_F32) * keep
    prev1 = uh[halo - 1:halo, :]
    prev2 = uh[halo - 2:halo - 1, :]
    row = lax.broadcasted_iota(jnp.int32, u.shape, 0)
    u1 = jnp.where(row == 0, prev1, pltpu.roll(u, 1, 0))
    u2 = jnp.where(row == 0, prev2, jnp.where(row == 1, prev1, pltpu.roll(u, 2, 0)))
    cw = cw_ref[...]
    y = cw[0:1, :] * u2 + cw[1:2, :] * u1 + cw[2:3, :] * u
    yc = (gb_ref[...].astype(_F32) * y).astype(_BF16)

    y_conv = jnp.dot(yc, wco_ref[...], preferred_element_type=_F32)
    y_att = jnp.dot(att_ref[...], wao_ref[...], preferred_element_type=_F32)
    merged = (jax.nn.sigmoid(gconv_ref[...].astype(_F32)) * y_conv
              + jax.nn.sigmoid(gatt_ref[...].astype(_F32)) * y_att)
    h = x_ref[...] + jnp.dot(merged.astype(_BF16), wmix_ref[...], preferred_element_type=_F32)
    h_ref[...] = h
    hn_ref[...] = ((h * _rms_scale(h)) * g_ref[...]).astype(hn_ref.dtype)


def _mixer_out(proj, att, x, conv_w, w_co, w_ao, w_mix, g_ffn, layer, seq, tm=256, halo=16):
    t, d = x.shape
    c = conv_w.shape[2]
    aw = att.shape[1]
    assert seq % tm == 0 and tm % halo == 0
    gate_col = (3 * c + 3 * aw) // d
    resident = pl.Buffered(1)
    kernel = functools.partial(_mixer_kernel, tiles_per_seq=seq // tm)
    halo_idx = lambda i: jnp.maximum(i * (tm // halo) - 1, 0)
    return pl.pallas_call(
        kernel,
        grid=(t // tm,),
        in_specs=[
            pl.BlockSpec((tm, c), lambda i: (i, 0)),
            pl.BlockSpec((tm, c), lambda i: (i, 1)),
            pl.BlockSpec((tm, c), lambda i: (i, 2)),
            pl.BlockSpec((halo, c), lambda i: (halo_idx(i), 0)),
            pl.BlockSpec((halo, c), lambda i: (halo_idx(i), 2)),
            pl.BlockSpec((tm, d), lambda i: (i, gate_col)),
            pl.BlockSpec((tm, d), lambda i: (i, gate_col + 1)),
            pl.BlockSpec((tm, aw), lambda i: (i, 0)),
            pl.BlockSpec((tm, d), lambda i: (i, 0)),
            pl.BlockSpec((None, CONV_K, c), lambda i: (layer, 0, 0)),
            pl.BlockSpec((None, c, d), lambda i: (layer, 0, 0), pipeline_mode=resident),
            pl.BlockSpec((None, aw, d), lambda i: (layer, 0, 0), pipeline_mode=resident),
            pl.BlockSpec((None, d, d), lambda i: (layer, 0, 0), pipeline_mode=resident),
            pl.BlockSpec((1, d), lambda i: (0, 0)),
        ],
        out_specs=[pl.BlockSpec((tm, d), lambda i: (i, 0)),
                   pl.BlockSpec((tm, d), lambda i: (i, 0))],
        out_shape=[jax.ShapeDtypeStruct((t, d), _F32),
                   jax.ShapeDtypeStruct((t, d), _BF16)],
        compiler_params=_params(("parallel",)),
        name="mixer_out",
    )(proj, proj, proj, proj, proj, proj, proj, att, x, conv_w, w_co, w_ao, w_mix, g_ffn)


def _ffn_act_kernel(hn_ref, wg_ref, wu_ref, o_ref):
    hn = hn_ref[...]
    a = jnp.dot(hn, wg_ref[...], preferred_element_type=_F32)
    b = jnp.dot(hn, wu_ref[...], preferred_element_type=_F32)
    o_ref[...] = ((a * jax.nn.sigmoid(a)) * b).astype(o_ref.dtype)


def _ffn_act(hn, w_gate, w_up, layer, tm=2048, tn=512):
    t, d = hn.shape
    ff = w_gate.shape[2]
    w_spec = pl.BlockSpec((None, d, tn), lambda j, i: (layer, 0, j))
    return pl.pallas_call(
        _ffn_act_kernel,
        grid=(ff // tn, t // tm),
        in_specs=[pl.BlockSpec((tm, d), lambda j, i: (i, 0)), w_spec, w_spec],
        out_specs=pl.BlockSpec((tm, tn), lambda j, i: (i, j)),
        out_shape=jax.ShapeDtypeStruct((t, ff), _BF16),
        compiler_params=_params(("parallel", "parallel")),
        name="ffn_act",
    )(hn, w_gate, w_up)


def _ffn_down_kernel(act_ref, wd_ref, h_ref, g_ref, *out_refs):
    x = h_ref[...] + jnp.dot(act_ref[...], wd_ref[...], preferred_element_type=_F32)
    if len(out_refs) == 2:
        out_refs[0][...] = x
    xn_ref = out_refs[-1]
    xn_ref[...] = ((x * _rms_scale(x)) * g_ref[...]).astype(xn_ref.dtype)


def _ffn_down(act, h, w_down, g_next, layer, emit_x, xn_dtype, tm=256):
    t, d = h.shape
    ff = w_down.shape[1]
    row_spec = pl.BlockSpec((tm, d), lambda i: (i, 0))
    out_specs = [row_spec, row_spec] if emit_x else [row_spec]
    out_shape = [jax.ShapeDtypeStruct((t, d), xn_dtype)]
    if emit_x:
        out_shape = [jax.ShapeDtypeStruct((t, d), _F32)] + out_shape
    return pl.pallas_call(
        _ffn_down_kernel,
        grid=(t // tm,),
        in_specs=[
            pl.BlockSpec((tm, ff), lambda i: (i, 0)),
            pl.BlockSpec((None, ff, d), lambda i: (layer, 0, 0), pipeline_mode=pl.Buffered(1)),
            row_spec,
            pl.BlockSpec((1, d), lambda i: (0, 0)),
        ],
        out_specs=out_specs,
        out_shape=out_shape,
        compiler_params=_params(("parallel",)),
        name="ffn_down",
    )(act, w_down, h, g_next)


def kernel(x, w_in, conv_w, w_conv_out, w_attn_out, w_mix_out, rel_bias, norm_mix, norm_ffn,
           w_ffn_gate, w_ffn_up, w_ffn_down, norm_final):
    batch, seq, d = x.shape
    depth = w_in.shape[0]
    c = conv_w.shape[2]
    aw = N_HEADS * HEAD_DIM
    t = batch * seq
    assert seq % MOBA_BLOCK == 0 and seq // MOBA_BLOCK >= MOBA_TOPK
    assert w_in.shape[2] == 3 * c + 3 * aw + 2 * d

    w_in, w_co, w_ao, w_mix = (w.astype(_BF16) for w in (w_in, w_conv_out, w_attn_out, w_mix_out))
    w_gate, w_up, w_down = (w.astype(_BF16) for w in (w_ffn_gate, w_ffn_up, w_ffn_down))
    bias_own, bias_prev = _bias_tables(rel_bias)
    q_col = 3 * c

    x = x.reshape(t, d)
    xn = _rmsnorm(x, norm_mix[0:1])
    for layer in range(depth):
        proj = _in_proj(xn, w_in, layer)
        att = _moba_attention(proj, bias_own, bias_prev, batch, seq, q_col)
        h, hn = _mixer_out(proj, att, x, conv_w, w_co, w_ao, w_mix, norm_ffn[layer:layer + 1],
                           layer, seq)
        act = _ffn_act(hn, w_gate, w_up, layer)
        if layer + 1 < depth:
            x, xn = _ffn_down(act, h, w_down, norm_mix[layer + 1:layer + 2], layer,
                              emit_x=True, xn_dtype=_BF16)
        else:
            (out,) = _ffn_down(act, h, w_down, norm_final.reshape(1, d), layer,
                               emit_x=False, xn_dtype=_F32)
    return out.reshape(batch, seq, d)
```

```python
import functools
import math

import numpy as np
import jax
import jax.numpy as jnp
from jax import lax
from jax.experimental import pallas as pl
from jax.experimental.pallas import tpu as pltpu

CONV_K = 3
N_HEADS = 8
HEAD_DIM = 128
MOBA_BLOCK = 256
MOBA_TOPK = 3
NUM_BUCKETS = 32
MAX_DISTANCE = 128
EPS = 1e-6
_LOG2_E = math.log2(math.e)

_BF16 = jnp.bfloat16
_F32 = jnp.float32
_V7X_VMEM_LIMIT_BYTES = 56 * 1024 * 1024


def _params(semantics):
    return pltpu.CompilerParams(dimension_semantics=semantics,
                                vmem_limit_bytes=_V7X_VMEM_LIMIT_BYTES)


def _rms_scale(x):
    return lax.rsqrt(jnp.mean(x * x, axis=-1, keepdims=True) + EPS)


def _norm_kernel(x_ref, g_ref, o_ref):
    x = x_ref[...]
    o_ref[...] = ((x * _rms_scale(x)) * g_ref[...]).astype(o_ref.dtype)


def _rmsnorm(x, g, tm=512):
    t, d = x.shape
    return pl.pallas_call(
        _norm_kernel,
        grid=(t // tm,),
        in_specs=[pl.BlockSpec((tm, d), lambda i: (i, 0)),
                  pl.BlockSpec((1, d), lambda i: (0, 0))],
        out_specs=pl.BlockSpec((tm, d), lambda i: (i, 0)),
        out_shape=jax.ShapeDtypeStruct((t, d), _BF16),
        compiler_params=_params(("parallel",)),
        name="rmsnorm",
    )(x, g)


def _matmul_kernel(x_ref, w_ref, o_ref):
    o_ref[...] = jnp.dot(x_ref[...], w_ref[...],
                         preferred_element_type=_F32).astype(o_ref.dtype)


def _in_proj(xn, w_in, layer, tm=1024, tn=2048):
    t, d = xn.shape
    n = w_in.shape[2]
    return pl.pallas_call(
        _matmul_kernel,
        grid=(t // tm, n // tn),
        in_specs=[pl.BlockSpec((tm, d), lambda i, j: (i, 0)),
                  pl.BlockSpec((None, d, tn), lambda i, j: (layer, 0, j))],
        out_specs=pl.BlockSpec((tm, tn), lambda i, j: (i, j)),
        out_shape=jax.ShapeDtypeStruct((t, n), _BF16),
        compiler_params=_params(("parallel", "parallel")),
        name="in_proj",
    )(xn, w_in)


def _t5_bucket_np(dist):
    n = np.maximum(dist, 0)
    max_exact = NUM_BUCKETS // 2
    nf = np.maximum(n, 1).astype(np.float64)
    large = max_exact + (np.log(nf / max_exact) / math.log(MAX_DISTANCE / max_exact)
                         * (NUM_BUCKETS - max_exact)).astype(np.int32)
    large = np.minimum(large, NUM_BUCKETS - 1)
    return np.where(n < max_exact, n, large).astype(np.int32)


def _bias_tables(rel_bias):
    blk = MOBA_BLOCK
    width = 2 * blk
    e = np.arange(width)
    d_own, d_prev = e - (blk - 1), e + 1
    idx = np.stack([_t5_bucket_np(d_own), _t5_bucket_np(d_prev)])
    table_t = rel_bias.astype(_F32).T
    heads = table_t.shape[0]
    vals = jnp.take(table_t, jnp.asarray(idx), axis=1)
    vals = vals - table_t[:, NUM_BUCKETS - 1][:, None, None]
    causal = np.stack([d_own >= 0, np.ones(width, bool)])
    vals = jnp.where(jnp.asarray(causal)[None], vals, -jnp.inf)
    skew = jnp.broadcast_to(vals[:, :, None, :], (heads, 2, blk, width))
    skew = skew.reshape(heads, 2, blk * width)[:, :, :blk * (width - 1)]
    skew = skew.reshape(heads, 2, blk, width - 1)[:, :, :, blk - 1:width - 1]
    skew = skew * _LOG2_E
    return skew[:, 0], skew[:, 1]


def _attn_kernel(q_ref, k_ref, v_ref, own_ref, prev_ref, o_ref,
                 kmean_ref, vt_ref, qbd_ref, mask_ref, m_ref, acc_ref, raw_ref, pend_p_ref, pend_a_ref,
                 *, nb, heads):
    i = pl.program_id(1)
    blk = MOBA_BLOCK
    dh = HEAD_DIM
    pairs = heads // 2
    log2_scale = dh ** -0.5 * _LOG2_E

    @pl.when(i == 0)
    def _():
        @pl.loop(0, nb)
        def _(n):
            rows = pl.ds(pl.multiple_of(n * blk, blk), blk)
            for h in range(heads):
                cols = slice(h * dh, (h + 1) * dh)
                kf = k_ref[rows, cols].astype(_F32)
                kmean_ref[h, pl.ds(n, 1), :] = jnp.sum(kf, axis=0, keepdims=True) * (1.0 / blk)
                vt_ref[h, n, 0:dh, :] = v_ref[rows, cols].T
                vt_ref[h, n, dh:, :] = jnp.ones((vt_ref.shape[2] - dh, blk), vt_ref.dtype)

    blk_id = lax.broadcasted_iota(jnp.int32, (nb, blk), 0)
    past = blk_id < i
    zeros = jnp.zeros((dh, blk), qbd_ref.dtype)
    for h in range(heads):
        g, half = divmod(h, 2)
        lanes = slice(half * blk, (half + 1) * blk)
        qf = q_ref[:, h * dh:(h + 1) * dh].T.astype(_F32)
        qs = (qf * log2_scale).astype(qbd_ref.dtype)
        qp = qf.astype(qbd_ref.dtype)
        for v, q in enumerate((qp, qs)):
            qbd_ref[v, g, half * dh:(half + 1) * dh, :] = (
                jnp.concatenate([q, zeros] if half == 0 else [zeros, q], axis=1))
        gate = jnp.dot(kmean_ref[h], qf, precision=lax.Precision.HIGHEST,
                       preferred_element_type=_F32)
        gate = jnp.where(past, gate, -jnp.inf)
        rank = jnp.zeros((nb, blk), jnp.int32)
        for n in range(nb):
            row = gate[n:n + 1, :]
            beats = (row > gate) | ((row == gate) & (n < blk_id))
            rank = rank + beats.astype(jnp.int32)
        mask_ref[g, :, lanes] = jnp.where((rank < MOBA_TOPK) & past, 0.0, -jnp.inf)

    ahead = raw_ref.shape[0]
    defer = pend_p_ref.shape[0]

    def qk(j, g, scaled):
        start = j * blk if isinstance(j, int) else pl.multiple_of(j * blk, blk)
        return jnp.dot(k_ref[pl.ds(start, blk), g * 2 * dh:(g + 1) * 2 * dh],
                       qbd_ref[int(scaled), g], preferred_element_type=_F32)

    def accumulate(g, j, p, alpha):
        for half in range(2):
            h = 2 * g + half
            lanes = slice(half * blk, (half + 1) * blk)
            pv = jnp.dot(vt_ref[h, j], p[:, lanes], preferred_element_type=_F32)
            acc_ref[h] = pv if alpha is None else alpha[:, lanes] * acc_ref[h] + pv

    def flush(pend_j, pend_first):
        for d in range(defer):
            accumulate(pairs - defer + d, pend_j, pend_p_ref[d],
                       None if pend_first else pend_a_ref[d])

    def stage(j, scores_of_pair, first, raw, next_j, pending, scaled, next_scaled):
        if pending is not None:
            flush(*pending)
        raw = [qk(j, g, scaled) for g in range(ahead)] if raw is None else list(raw)
        raw_next = []
        for g in range(pairs):
            if g + ahead < pairs:
                raw.append(qk(j, g + ahead, scaled))
            else:
                raw_next.append(qk(next_j, g + ahead - pairs, next_scaled))
            s = scores_of_pair(g, raw[g])
            m_new = jnp.max(s, axis=0, keepdims=True)
            alpha = None
            if not first:
                m_old = m_ref[g]
                m_new = jnp.maximum(m_old, m_new)
                alpha = jnp.exp2(m_old - m_new)
            m_ref[g] = m_new
            p = jnp.exp2(s - m_new).astype(vt_ref.dtype)
            if g >= pairs - defer:
                pend_p_ref[g - (pairs - defer)] = p
                if not first:
                    pend_a_ref[g - (pairs - defer)] = alpha
            else:
                accumulate(g, j, p, alpha)
        return raw_next

    jp = jnp.maximum(i - 1, 0)
    raw = stage(i, lambda g, r: r * log2_scale + own_ref[g], True, None, jp, None, False, False)
    raw = stage(jp, lambda g, r: r * log2_scale + (prev_ref[g] + mask_ref[g, pl.ds(jp, 1), :]), False,
                raw, 0, (i, True), False, True)
    for a in range(ahead):
        raw_ref[a] = raw[a]

    @pl.loop(0, jp)
    def _(j):
        raw = stage(j, lambda g, r: jnp.where(mask_ref[g, pl.ds(j, 1), :] == 0.0, r, -jnp.inf), False,
                    [raw_ref[a] for a in range(ahead)], j + 1,
                    (jnp.where(j == 0, jp, j - 1), False), True, True)
        for a in range(ahead):
            raw_ref[a] = raw[a]

    flush(jnp.maximum(jp - 1, 0), False)
    for h in range(heads):
        acc = acc_ref[h]
        out = acc[0:dh, :] * (1.0 / acc[dh:dh + 1, :])
        o_ref[:, h * dh:(h + 1) * dh] = out.T.astype(o_ref.dtype)


def _moba_attention(proj, bias_own, bias_prev, batch, seq, q_col):
    t = proj.shape[0]
    nb = seq // MOBA_BLOCK
    blk = MOBA_BLOCK
    heads = N_HEADS
    pairs = heads // 2
    w = heads * HEAD_DIM
    q_blk = q_col // w
    ones_rows = 16
    once = pl.Buffered(1)

    def pair_layout(bias):
        return bias.reshape(pairs, 2, blk, blk).transpose(0, 2, 1, 3).reshape(pairs, blk, 2 * blk)

    kernel = functools.partial(_attn_kernel, nb=nb, heads=heads)
    return pl.pallas_call(
        kernel,
        grid=(batch, nb),
        in_specs=[
            pl.BlockSpec((blk, w), lambda b, i: (b * nb + i, q_blk)),
            pl.BlockSpec((seq, w), lambda b, i: (b, q_blk + 1), pipeline_mode=once),
            pl.BlockSpec((seq, w), lambda b, i: (b, q_blk + 2), pipeline_mode=once),
            pl.BlockSpec((pairs, blk, 2 * blk), lambda b, i: (0, 0, 0), pipeline_mode=once),
            pl.BlockSpec((pairs, blk, 2 * blk), lambda b, i: (0, 0, 0), pipeline_mode=once),
        ],
        out_specs=pl.BlockSpec((blk, w), lambda b, i: (b * nb + i, 0)),
        out_shape=jax.ShapeDtypeStruct((t, w), _BF16),
        scratch_shapes=[
            pltpu.VMEM((heads, nb, HEAD_DIM), _F32),
            pltpu.VMEM((heads, nb, HEAD_DIM + ones_rows, blk), _BF16),
            pltpu.VMEM((2, pairs, 2 * HEAD_DIM, 2 * blk), _BF16),
            pltpu.VMEM((pairs, nb, 2 * blk), _F32),
            pltpu.VMEM((pairs, 1, 2 * blk), _F32),
            pltpu.VMEM((heads, HEAD_DIM + ones_rows, blk), _F32),
            pltpu.VMEM((pairs // 2, blk, 2 * blk), _F32),
            pltpu.VMEM((1, blk, 2 * blk), _BF16),
            pltpu.VMEM((1, 1, 2 * blk), _F32),
        ],
        compiler_params=_params(("parallel", "arbitrary")),
        name="moba_attention",
    )(proj, proj, proj, pair_layout(bias_own), pair_layout(bias_prev))


def _mixer_kernel(hin_ref, gb_ref, gc_ref, hin_halo_ref, gc_halo_ref, gconv_ref, gatt_ref,
                  att_ref, x_ref, cw_ref, wco_ref, wao_ref, wmix_ref, g_ref,
                  h_ref, hn_ref, *, tiles_per_seq):
    i = pl.program_id(0)
    tm = hin_ref.shape[0]
    halo = hin_halo_ref.shape[0]

    u = gc_ref[...].astype(_F32) * hin_ref[...].astype(_F32)
    keep = (i % tiles_per_seq != 0).astype(_F32)
    uh = gc_halo_ref[...].astype(_F32) * hin_halo_ref[...].astype(_F32) * keep
    prev1 = uh[halo - 1:halo, :]
    prev2 = uh[halo - 2:halo - 1, :]
    row = lax.broadcasted_iota(jnp.int32, u.shape, 0)
    u1 = jnp.where(row == 0, prev1, pltpu.roll(u, 1, 0))
    u2 = jnp.where(row == 0, prev2, jnp.where(row == 1, prev1, pltpu.roll(u, 2, 0)))
    cw = cw_ref[...]
    y = cw[0:1, :] * u2 + cw[1:2, :] * u1 + cw[2:3, :] * u
    yc = (gb_ref[...].astype(_F32) * y).astype(_BF16)

    y_conv = jnp.dot(yc, wco_ref[...], preferred_element_type=_F32)
    y_att = jnp.dot(att_ref[...], wao_ref[...], preferred_element_type=_F32)
    merged = (jax.nn.sigmoid(gconv_ref[...].astype(_F32)) * y_conv
              + jax.nn.sigmoid(gatt_ref[...].astype(_F32)) * y_att)
    h = x_ref[...] + jnp.dot(merged.astype(_BF16), wmix_ref[...], preferred_element_type=_F32)
    h_ref[...] = h
    hn_ref[...] = ((h * _rms_scale(h)) * g_ref[...]).astype(hn_ref.dtype)


def _mixer_out(proj, att, x, conv_w, w_co, w_ao, w_mix, g_ffn, layer, seq, tm=256, halo=16):
    t, d = x.shape
    c = conv_w.shape[2]
    aw = att.shape[1]
    assert seq % tm == 0 and tm % halo == 0
    gate_col = (3 * c + 3 * aw) // d
    resident = pl.Buffered(1)
    kernel = functools.partial(_mixer_kernel, tiles_per_seq=seq // tm)
    halo_idx = lambda i: jnp.maximum(i * (tm // halo) - 1, 0)
    return pl.pallas_call(
        kernel,
        grid=(t // tm,),
        in_specs=[
            pl.BlockSpec((tm, c), lambda i: (i, 0)),
            pl.BlockSpec((tm, c), lambda i: (i, 1)),
            pl.BlockSpec((tm, c), lambda i: (i, 2)),
            pl.BlockSpec((halo, c), lambda i: (halo_idx(i), 0)),
            pl.BlockSpec((halo, c), lambda i: (halo_idx(i), 2)),
            pl.BlockSpec((tm, d), lambda i: (i, gate_col)),
            pl.BlockSpec((tm, d), lambda i: (i, gate_col + 1)),
            pl.BlockSpec((tm, aw), lambda i: (i, 0)),
            pl.BlockSpec((tm, d), lambda i: (i, 0)),
            pl.BlockSpec((None, CONV_K, c), lambda i: (layer, 0, 0)),
            pl.BlockSpec((None, c, d), lambda i: (layer, 0, 0), pipeline_mode=resident),
            pl.BlockSpec((None, aw, d), lambda i: (layer, 0, 0), pipeline_mode=resident),
            pl.BlockSpec((None, d, d), lambda i: (layer, 0, 0), pipeline_mode=resident),
            pl.BlockSpec((1, d), lambda i: (0, 0)),
        ],
        out_specs=[pl.BlockSpec((tm, d), lambda i: (i, 0)),
                   pl.BlockSpec((tm, d), lambda i: (i, 0))],
        out_shape=[jax.ShapeDtypeStruct((t, d), _F32),
                   jax.ShapeDtypeStruct((t, d), _BF16)],
        compiler_params=_params(("parallel",)),
        name="mixer_out",
    )(proj, proj, proj, proj, proj, proj, proj, att, x, conv_w, w_co, w_ao, w_mix, g_ffn)


def _ffn_act_kernel(hn_ref, wg_ref, wu_ref, o_ref):
    hn = hn_ref[...]
    a = jnp.dot(hn, wg_ref[...], preferred_element_type=_F32)
    b = jnp.dot(hn, wu_ref[...], preferred_element_type=_F32)
    o_ref[...] = ((a * jax.nn.sigmoid(a)) * b).astype(o_ref.dtype)


def _ffn_act(hn, w_gate, w_up, layer, tm=2048, tn=512):
    t, d = hn.shape
    ff = w_gate.shape[2]
    w_spec = pl.BlockSpec((None, d, tn), lambda j, i: (layer, 0, j))
    return pl.pallas_call(
        _ffn_act_kernel,
        grid=(ff // tn, t // tm),
        in_specs=[pl.BlockSpec((tm, d), lambda j, i: (i, 0)), w_spec, w_spec],
        out_specs=pl.BlockSpec((tm, tn), lambda j, i: (i, j)),
        out_shape=jax.ShapeDtypeStruct((t, ff), _BF16),
        compiler_params=_params(("parallel", "parallel")),
        name="ffn_act",
    )(hn, w_gate, w_up)


def _ffn_down_kernel(act_ref, wd_ref, h_ref, g_ref, *out_refs):
    x = h_ref[...] + jnp.dot(act_ref[...], wd_ref[...], preferred_element_type=_F32)
    if len(out_refs) == 2:
        out_refs[0][...] = x
    xn_ref = out_refs[-1]
    xn_ref[...] = ((x * _rms_scale(x)) * g_ref[...]).astype(xn_ref.dtype)


def _ffn_down(act, h, w_down, g_next, layer, emit_x, xn_dtype, tm=256):
    t, d = h.shape
    ff = w_down.shape[1]
    row_spec = pl.BlockSpec((tm, d), lambda i: (i, 0))
    out_specs = [row_spec, row_spec] if emit_x else [row_spec]
    out_shape = [jax.ShapeDtypeStruct((t, d), xn_dtype)]
    if emit_x:
        out_shape = [jax.ShapeDtypeStruct((t, d), _F32)] + out_shape
    return pl.pallas_call(
        _ffn_down_kernel,
        grid=(t // tm,),
        in_specs=[
            pl.BlockSpec((tm, ff), lambda i: (i, 0)),
            pl.BlockSpec((None, ff, d), lambda i: (layer, 0, 0), pipeline_mode=pl.Buffered(1)),
            row_spec,
            pl.BlockSpec((1, d), lambda i: (0, 0)),
        ],
        out_specs=out_specs,
        out_shape=out_shape,
        compiler_params=_params(("parallel",)),
        name="ffn_down",
    )(act, w_down, h, g_next)


def kernel(x, w_in, conv_w, w_conv_out, w_attn_out, w_mix_out, rel_bias, norm_mix, norm_ffn,
           w_ffn_gate, w_ffn_up, w_ffn_down, norm_final):
    batch, seq, d = x.shape
    depth = w_in.shape[0]
    c = conv_w.shape[2]
    aw = N_HEADS * HEAD_DIM
    t = batch * seq
    assert seq % MOBA_BLOCK == 0 and seq // MOBA_BLOCK >= MOBA_TOPK
    assert w_in.shape[2] == 3 * c + 3 * aw + 2 * d

    w_in, w_co, w_ao, w_mix = (w.astype(_BF16) for w in (w_in, w_conv_out, w_attn_out, w_mix_out))
    w_gate, w_up, w_down = (w.astype(_BF16) for w in (w_ffn_gate, w_ffn_up, w_ffn_down))
    bias_own, bias_prev = _bias_tables(rel_bias)
    q_col = 3 * c

    x = x.reshape(t, d)
    xn = _rmsnorm(x, norm_mix[0:1])
    for layer in range(depth):
        proj = _in_proj(xn, w_in, layer)
        att = _moba_attention(proj, bias_own, bias_prev, batch, seq, q_col)
        h, hn = _mixer_out(proj, att, x, conv_w, w_co, w_ao, w_mix, norm_ffn[layer:layer + 1],
                           layer, seq)
        act = _ffn_act(hn, w_gate, w_up, layer)
        if layer + 1 < depth:
            x, xn = _ffn_down(act, h, w_down, norm_mix[layer + 1:layer + 2], layer,
                              emit_x=True, xn_dtype=_BF16)
        else:
            (out,) = _ffn_down(act, h, w_down, norm_final.reshape(1, d), layer,
                               emit_x=False, xn_dtype=_F32)
    return out.reshape(batch, seq, d)
```

```python
import functools
import math

import numpy as np
import jax
import jax.numpy as jnp
from jax import lax
from jax.experimental import pallas as pl
from jax.experimental.pallas import tpu as pltpu

CONV_K = 3
N_HEADS = 8
HEAD_DIM = 128
MOBA_BLOCK = 256
MOBA_TOPK = 3
NUM_BUCKETS = 32
MAX_DISTANCE = 128
EPS = 1e-6
_LOG2_E = math.log2(math.e)

_BF16 = jnp.bfloat16
_F32 = jnp.float32
_V7X_VMEM_LIMIT_BYTES = 56 * 1024 * 1024


def _params(semantics):
    return pltpu.CompilerParams(dimension_semantics=semantics,
                                vmem_limit_bytes=_V7X_VMEM_LIMIT_BYTES)


def _rms_scale(x):
    return lax.rsqrt(jnp.mean(x * x, axis=-1, keepdims=True) + EPS)


def _norm_kernel(x_ref, g_ref, o_ref):
    x = x_ref[...]
    o_ref[...] = ((x * _rms_scale(x)) * g_ref[...]).astype(o_ref.dtype)


def _rmsnorm(x, g, tm=512):
    t, d = x.shape
    return pl.pallas_call(
        _norm_kernel,
        grid=(t // tm,),
        in_specs=[pl.BlockSpec((tm, d), lambda i: (i, 0)),
                  pl.BlockSpec((1, d), lambda i: (0, 0))],
        out_specs=pl.BlockSpec((tm, d), lambda i: (i, 0)),
        out_shape=jax.ShapeDtypeStruct((t, d), _BF16),
        compiler_params=_params(("parallel",)),
        name="rmsnorm",
    )(x, g)


def _matmul_kernel(x_ref, w_ref, o_ref):
    o_ref[...] = jnp.dot(x_ref[...], w_ref[...],
                         preferred_element_type=_F32).astype(o_ref.dtype)


def _in_proj(xn, w_in, layer, tm=1024, tn=2048):
    t, d = xn.shape
    n = w_in.shape[2]
    return pl.pallas_call(
        _matmul_kernel,
        grid=(t // tm, n // tn),
        in_specs=[pl.BlockSpec((tm, d), lambda i, j: (i, 0)),
                  pl.BlockSpec((None, d, tn), lambda i, j: (layer, 0, j))],
        out_specs=pl.BlockSpec((tm, tn), lambda i, j: (i, j)),
        out_shape=jax.ShapeDtypeStruct((t, n), _BF16),
        compiler_params=_params(("parallel", "parallel")),
        name="in_proj",
    )(xn, w_in)


def _t5_bucket_np(dist):
    n = np.maximum(dist, 0)
    max_exact = NUM_BUCKETS // 2
    nf = np.maximum(n, 1).astype(np.float64)
    large = max_exact + (np.log(nf / max_exact) / math.log(MAX_DISTANCE / max_exact)
                         * (NUM_BUCKETS - max_exact)).astype(np.int32)
    large = np.minimum(large, NUM_BUCKETS - 1)
    return np.where(n < max_exact, n, large).astype(np.int32)


def _bias_tables(rel_bias):
    blk = MOBA_BLOCK
    width = 2 * blk
    e = np.arange(width)
    d_own, d_prev = e - (blk - 1), e + 1
    idx = np.stack([_t5_bucket_np(d_own), _t5_bucket_np(d_prev)])
    table_t = rel_bias.astype(_F32).T
    heads = table_t.shape[0]
    vals = jnp.take(table_t, jnp.asarray(idx), axis=1)
    vals = vals - table_t[:, NUM_BUCKETS - 1][:, None, None]
    causal = np.stack([d_own >= 0, np.ones(width, bool)])
    vals = jnp.where(jnp.asarray(causal)[None], vals, -jnp.inf)
    skew = jnp.broadcast_to(vals[:, :, None, :], (heads, 2, blk, width))
    skew = skew.reshape(heads, 2, blk * width)[:, :, :blk * (width - 1)]
    skew = skew.reshape(heads, 2, blk, width - 1)[:, :, :, blk - 1:width - 1]
    skew = skew * _LOG2_E
    return skew[:, 0], skew[:, 1]


def _attn_kernel(q_ref, k_ref, v_ref, own_ref, prev_ref, o_ref,
                 kmean_ref, vt_ref, qbd_ref, mask_ref, m_ref, acc_ref, raw_ref, pend_p_ref, pend_a_ref,
                 *, nb, heads):
    i = pl.program_id(1)
    blk = MOBA_BLOCK
    dh = HEAD_DIM
    pairs = heads // 2
    log2_scale = dh ** -0.5 * _LOG2_E

    @pl.when(i == 0)
    def _():
        @pl.loop(0, nb)
        def _(n):
            rows = pl.ds(pl.multiple_of(n * blk, blk), blk)
            for h in range(heads):
                cols = slice(h * dh, (h + 1) * dh)
                kf = k_ref[rows, cols].astype(_F32)
                kmean_ref[h, pl.ds(n, 1), :] = jnp.sum(kf, axis=0, keepdims=True) * (1.0 / blk)
                vt_ref[h, n, 0:dh, :] = v_ref[rows, cols].T
                vt_ref[h, n, dh:, :] = jnp.ones((vt_ref.shape[2] - dh, blk), vt_ref.dtype)

    blk_id = lax.broadcasted_iota(jnp.int32, (nb, blk), 0)
    past = blk_id < i
    zeros = jnp.zeros((dh, blk), qbd_ref.dtype)
    for h in range(heads):
        g, half = divmod(h, 2)
        lanes = slice(half * blk, (half + 1) * blk)
        qf = q_ref[:, h * dh:(h + 1) * dh].T.astype(_F32)
        qs = (qf * log2_scale).astype(qbd_ref.dtype)
        qp = qf.astype(qbd_ref.dtype)
        for v, q in enumerate((qp, qs)):
            qbd_ref[v, g, half * dh:(half + 1) * dh, :] = (
                jnp.concatenate([q, zeros] if half == 0 else [zeros, q], axis=1))
        gate = jnp.dot(kmean_ref[h], qf, precision=lax.Precision.HIGHEST,
                       preferred_element_type=_F32)
        gate = jnp.where(past, gate, -jnp.inf)
        rank = jnp.zeros((nb, blk), jnp.int32)
        for n in range(nb):
            row = gate[n:n + 1, :]
            beats = (row > gate) | ((row == gate) & (n < blk_id))
            rank = rank + beats.astype(jnp.int32)
        mask_ref[g, :, lanes] = jnp.where((rank < MOBA_TOPK) & past, 0.0, -jnp.inf)

    ahead = raw_ref.shape[0]
    defer = pend_p_ref.shape[0]

    def qk(j, g, scaled):
        start = j * blk if isinstance(j, int) else pl.multiple_of(j * blk, blk)
        return jnp.dot(k_ref[pl.ds(start, blk), g * 2 * dh:(g + 1) * 2 * dh],
                       qbd_ref[int(scaled), g], preferred_element_type=_F32)

    def accumulate(g, j, p, alpha):
        for half in range(2):
            h = 2 * g + half
            lanes = slice(half * blk, (half + 1) * blk)
            pv = jnp.dot(vt_ref[h, j], p[:, lanes], preferred_element_type=_F32)
            acc_ref[h] = pv if alpha is None else alpha[:, lanes] * acc_ref[h] + pv

    def flush(pend_j, pend_first):
        for d in range(defer):
            accumulate(pairs - defer + d, pend_j, pend_p_ref[d],
                       None if pend_first else pend_a_ref[d])

    def stage(j, scores_of_pair, first, raw, next_j, pending, scaled, next_scaled):
        if pending is not None:
            flush(*pending)
        raw = [qk(j, g, scaled) for g in range(ahead)] if raw is None else list(raw)
        raw_next = []
        for g in range(pairs):
            if g + ahead < pairs:
                raw.append(qk(j, g + ahead, scaled))
            else:
                raw_next.append(qk(next_j, g + ahead - pairs, next_scaled))
            s = scores_of_pair(g, raw[g])
            m_new = jnp.max(s, axis=0, keepdims=True)
            alpha = None
            if not first:
                m_old = m_ref[g]
                m_new = jnp.maximum(m_old, m_new)
                alpha = jnp.exp2(m_old - m_new)
            m_ref[g] = m_new
            p = jnp.exp2(s - m_new).astype(vt_ref.dtype)
            if g >= pairs - defer:
                pend_p_ref[g - (pairs - defer)] = p
                if not first:
                    pend_a_ref[g - (pairs - defer)] = alpha
            else:
                accumulate(g, j, p, alpha)
        return raw_next

    jp = jnp.maximum(i - 1, 0)
    raw = stage(i, lambda g, r: r * log2_scale + own_ref[g], True, None, jp, None, False, False)
    raw = stage(jp, lambda g, r: r * log2_scale + (prev_ref[g] + mask_ref[g, pl.ds(jp, 1), :]), False,
                raw, 0, (i, True), False, True)
    for a in range(ahead):
        raw_ref[a] = raw[a]

    @pl.loop(0, jp)
    def _(j):
        raw = stage(j, lambda g, r: jnp.where(mask_ref[g, pl.ds(j, 1), :] == 0.0, r, -jnp.inf), False,
                    [raw_ref[a] for a in range(ahead)], j + 1,
                    (jnp.where(j == 0, jp, j - 1), False), True, True)
        for a in range(ahead):
            raw_ref[a] = raw[a]

    flush(jnp.maximum(jp - 1, 0), False)
    for h in range(heads):
        acc = acc_ref[h]
        out = acc[0:dh, :] * (1.0 / acc[dh:dh + 1, :])
        o_ref[:, h * dh:(h + 1) * dh] = out.T.astype(o_ref.dtype)


def _moba_attention(proj, bias_own, bias_prev, batch, seq, q_col):
    t = proj.shape[0]
    nb = seq // MOBA_BLOCK
    blk = MOBA_BLOCK
    heads = N_HEADS
    pairs = heads // 2
    w = heads * HEAD_DIM
    q_blk = q_col // w
    ones_rows = 16
    once = pl.Buffered(1)

    def pair_layout(bias):
        return bias.reshape(pairs, 2, blk, blk).transpose(0, 2, 1, 3).reshape(pairs, blk, 2 * blk)

    kernel = functools.partial(_attn_kernel, nb=nb, heads=heads)
    return pl.pallas_call(
        kernel,
        grid=(batch, nb),
        in_specs=[
            pl.BlockSpec((blk, w), lambda b, i: (b * nb + i, q_blk)),
            pl.BlockSpec((seq, w), lambda b, i: (b, q_blk + 1)),
            pl.BlockSpec((seq, w), lambda b, i: (b, q_blk + 2)),
            pl.BlockSpec((pairs, blk, 2 * blk), lambda b, i: (0, 0, 0), pipeline_mode=once),
            pl.BlockSpec((pairs, blk, 2 * blk), lambda b, i: (0, 0, 0), pipeline_mode=once),
        ],
        out_specs=pl.BlockSpec((blk, w), lambda b, i: (b * nb + i, 0)),
        out_shape=jax.ShapeDtypeStruct((t, w), _BF16),
        scratch_shapes=[
            pltpu.VMEM((heads, nb, HEAD_DIM), _F32),
            pltpu.VMEM((heads, nb, HEAD_DIM + ones_rows, blk), _BF16),
            pltpu.VMEM((2, pairs, 2 * HEAD_DIM, 2 * blk), _BF16),
            pltpu.VMEM((pairs, nb, 2 * blk), _F32),
            pltpu.VMEM((pairs, 1, 2 * blk), _F32),
            pltpu.VMEM((heads, HEAD_DIM + ones_rows, blk), _F32),
            pltpu.VMEM((pairs // 2, blk, 2 * blk), _F32),
            pltpu.VMEM((1, blk, 2 * blk), _BF16),
            pltpu.VMEM((1, 1, 2 * blk), _F32),
        ],
        compiler_params=_params(("parallel", "arbitrary")),
        name="moba_attention",
    )(proj, proj, proj, pair_layout(bias_own), pair_layout(bias_prev))


def _mixer_kernel(hin_ref, gb_ref, gc_ref, hin_halo_ref, gc_halo_ref, gconv_ref, gatt_ref,
                  att_ref, x_ref, cw_ref, wco_ref, wao_ref, wmix_ref, g_ref,
                  h_ref, hn_ref, *, tiles_per_seq):
    i = pl.program_id(0)
    tm = hin_ref.shape[0]
    halo = hin_halo_ref.shape[0]

    u = gc_ref[...].astype(_F32) * hin_ref[...].astype(_F32)
    keep = (i % tiles_per_seq != 0).astype(_F32)
    uh = gc_halo_ref[...].astype(_F32) * hin_halo_ref[...].astype(_F32) * keep
    prev1 = uh[halo - 1:halo, :]
    prev2 = uh[halo - 2:halo - 1, :]
    row = lax.broadcasted_iota(jnp.int32, u.shape, 0)
    u1 = jnp.where(row == 0, prev1, pltpu.roll(u, 1, 0))
    u2 = jnp.where(row == 0, prev2, jnp.where(row == 1, prev1, pltpu.roll(u, 2, 0)))
    cw = cw_ref[...]
    y = cw[0:1, :] * u2 + cw[1:2, :] * u1 + cw[2:3, :] * u
    yc = (gb_ref[...].astype(_F32) * y).astype(_BF16)

    y_conv = jnp.dot(yc, wco_ref[...], preferred_element_type=_F32)
    y_att = jnp.dot(att_ref[...], wao_ref[...], preferred_element_type=_F32)
    merged = (jax.nn.sigmoid(gconv_ref[...].astype(_F32)) * y_conv
              + jax.nn.sigmoid(gatt_ref[...].astype(_F32)) * y_att)
    h = x_ref[...] + jnp.dot(merged.astype(_BF16), wmix_ref[...], preferred_element_type=_F32)
    h_ref[...] = h
    hn_ref[...] = ((h * _rms_scale(h)) * g_ref[...]).astype(hn_ref.dtype)


def _mixer_out(proj, att, x, conv_w, w_co, w_ao, w_mix, g_ffn, layer, seq, tm=256, halo=16):
    t, d = x.shape
    c = conv_w.shape[2]
    aw = att.shape[1]
    assert seq % tm == 0 and tm % halo == 0
    gate_col = (3 * c + 3 * aw) // d
    resident = pl.Buffered(1)
    kernel = functools.partial(_mixer_kernel, tiles_per_seq=seq // tm)
    halo_idx = lambda i: jnp.maximum(i * (tm // halo) - 1, 0)
    return pl.pallas_call(
        kernel,
        grid=(t // tm,),
        in_specs=[
            pl.BlockSpec((tm, c), lambda i: (i, 0)),
            pl.BlockSpec((tm, c), lambda i: (i, 1)),
            pl.BlockSpec((tm, c), lambda i: (i, 2)),
            pl.BlockSpec((halo, c), lambda i: (halo_idx(i), 0)),
            pl.BlockSpec((halo, c), lambda i: (halo_idx(i), 2)),
            pl.BlockSpec((tm, d), lambda i: (i, gate_col)),
            pl.BlockSpec((tm, d), lambda i: (i, gate_col + 1)),
            pl.BlockSpec((tm, aw), lambda i: (i, 0)),
            pl.BlockSpec((tm, d), lambda i: (i, 0)),
            pl.BlockSpec((None, CONV_K, c), lambda i: (layer, 0, 0)),
            pl.BlockSpec((None, c, d), lambda i: (layer, 0, 0), pipeline_mode=resident),
            pl.BlockSpec((None, aw, d), lambda i: (layer, 0, 0), pipeline_mode=resident),
            pl.BlockSpec((None, d, d), lambda i: (layer, 0, 0), pipeline_mode=resident),
            pl.BlockSpec((1, d), lambda i: (0, 0)),
        ],
        out_specs=[pl.BlockSpec((tm, d), lambda i: (i, 0)),
                   pl.BlockSpec((tm, d), lambda i: (i, 0))],
        out_shape=[jax.ShapeDtypeStruct((t, d), _F32),
                   jax.ShapeDtypeStruct((t, d), _BF16)],
        compiler_params=_params(("parallel",)),
        name="mixer_out",
    )(proj, proj, proj, proj, proj, proj, proj, att, x, conv_w, w_co, w_ao, w_mix, g_ffn)


def _ffn_act_kernel(hn_ref, wg_ref, wu_ref, o_ref, *, row_chunk):
    for r0 in range(0, hn_ref.shape[0], row_chunk):
        rows = slice(r0, r0 + row_chunk)
        hn = hn_ref[rows, :]
        a = jnp.dot(hn, wg_ref[...], preferred_element_type=_F32)
        b = jnp.dot(hn, wu_ref[...], preferred_element_type=_F32)
        o_ref[rows, :] = ((a * jax.nn.sigmoid(a)) * b).astype(o_ref.dtype)


def _ffn_act(hn, w_gate, w_up, layer, tm=2048, tn=512, row_chunk=256):
    t, d = hn.shape
    ff = w_gate.shape[2]
    w_spec = pl.BlockSpec((None, d, tn), lambda j, i: (layer, 0, j))
    return pl.pallas_call(
        functools.partial(_ffn_act_kernel, row_chunk=row_chunk),
        grid=(ff // tn, t // tm),
        in_specs=[pl.BlockSpec((tm, d), lambda j, i: (i, 0)), w_spec, w_spec],
        out_specs=pl.BlockSpec((tm, tn), lambda j, i: (i, j)),
        out_shape=jax.ShapeDtypeStruct((t, ff), _BF16),
        compiler_params=_params(("parallel", "parallel")),
        name="ffn_act",
    )(hn, w_gate, w_up)


def _ffn_down_kernel(act_ref, wd_ref, h_ref, g_ref, *out_refs):
    x = h_ref[...] + jnp.dot(act_ref[...], wd_ref[...], preferred_element_type=_F32)
    if len(out_refs) == 2:
        out_refs[0][...] = x
    xn_ref = out_refs[-1]
    xn_ref[...] = ((x * _rms_scale(x)) * g_ref[...]).astype(xn_ref.dtype)


def _ffn_down(act, h, w_down, g_next, layer, emit_x, xn_dtype, tm=256):
    t, d = h.shape
    ff = w_down.shape[1]
    row_spec = pl.BlockSpec((tm, d), lambda i: (i, 0))
    out_specs = [row_spec, row_spec] if emit_x else [row_spec]
    out_shape = [jax.ShapeDtypeStruct((t, d), xn_dtype)]
    if emit_x:
        out_shape = [jax.ShapeDtypeStruct((t, d), _F32)] + out_shape
    return pl.pallas_call(
        _ffn_down_kernel,
        grid=(t // tm,),
        in_specs=[
            pl.BlockSpec((tm, ff), lambda i: (i, 0)),
            pl.BlockSpec((None, ff, d), lambda i: (layer, 0, 0), pipeline_mode=pl.Buffered(1)),
            row_spec,
            pl.BlockSpec((1, d), lambda i: (0, 0)),
        ],
        out_specs=out_specs,
        out_shape=out_shape,
        compiler_params=_params(("parallel",)),
        name="ffn_down",
    )(act, w_down, h, g_next)


def kernel(x, w_in, conv_w, w_conv_out, w_attn_out, w_mix_out, rel_bias, norm_mix, norm_ffn,
           w_ffn_gate, w_ffn_up, w_ffn_down, norm_final):
    batch, seq, d = x.shape
    depth = w_in.shape[0]
    c = conv_w.shape[2]
    aw = N_HEADS * HEAD_DIM
    t = batch * seq
    assert seq % MOBA_BLOCK == 0 and seq // MOBA_BLOCK >= MOBA_TOPK
    assert w_in.shape[2] == 3 * c + 3 * aw + 2 * d

    w_in, w_co, w_ao, w_mix = (w.astype(_BF16) for w in (w_in, w_conv_out, w_attn_out, w_mix_out))
    w_gate, w_up, w_down = (w.astype(_BF16) for w in (w_ffn_gate, w_ffn_up, w_ffn_down))
    bias_own, bias_prev = _bias_tables(rel_bias)
    q_col = 3 * c

    x = x.reshape(t, d)
    xn = _rmsnorm(x, norm_mix[0:1])
    for layer in range(depth):
        proj = _in_proj(xn, w_in, layer)
        att = _moba_attention(proj, bias_own, bias_prev, batch, seq, q_col)
        h, hn = _mixer_out(proj, att, x, conv_w, w_co, w_ao, w_mix, norm_ffn[layer:layer + 1],
                           layer, seq)
        act = _ffn_act(hn, w_gate, w_up, layer)
        if layer + 1 < depth:
            x, xn = _ffn_down(act, h, w_down, norm_mix[layer + 1:layer + 2], layer,
                              emit_x=True, xn_dtype=_BF16)
        else:
            (out,) = _ffn_down(act, h, w_down, norm_final.reshape(1, d), layer,
                               emit_x=False, xn_dtype=_F32)
    return out.reshape(batch, seq, d)
```

```python
import functools
import math

import numpy as np
import jax
import jax.numpy as jnp
from jax import lax
from jax.experimental import pallas as pl
from jax.experimental.pallas import tpu as pltpu

CONV_K = 3
N_HEADS = 8
HEAD_DIM = 128
MOBA_BLOCK = 256
MOBA_TOPK = 3
NUM_BUCKETS = 32
MAX_DISTANCE = 128
EPS = 1e-6
_LOG2_E = math.log2(math.e)

_BF16 = jnp.bfloat16
_F32 = jnp.float32
_V7X_VMEM_LIMIT_BYTES = 56 * 1024 * 1024


def _params(semantics):
    return pltpu.CompilerParams(dimension_semantics=semantics,
                                vmem_limit_bytes=_V7X_VMEM_LIMIT_BYTES)


def _rms_scale(x):
    return lax.rsqrt(jnp.mean(x * x, axis=-1, keepdims=True) + EPS)


def _norm_kernel(x_ref, g_ref, o_ref):
    x = x_ref[...]
    o_ref[...] = ((x * _rms_scale(x)) * g_ref[...]).astype(o_ref.dtype)


def _rmsnorm(x, g, tm=512):
    t, d = x.shape
    return pl.pallas_call(
        _norm_kernel,
        grid=(t // tm,),
        in_specs=[pl.BlockSpec((tm, d), lambda i: (i, 0)),
                  pl.BlockSpec((1, d), lambda i: (0, 0))],
        out_specs=pl.BlockSpec((tm, d), lambda i: (i, 0)),
        out_shape=jax.ShapeDtypeStruct((t, d), _BF16),
        compiler_params=_params(("parallel",)),
        name="rmsnorm",
    )(x, g)


def _matmul_kernel(x_ref, w_ref, o_ref):
    o_ref[...] = jnp.dot(x_ref[...], w_ref[...],
                         preferred_element_type=_F32).astype(o_ref.dtype)


def _in_proj(xn, w_in, layer, tm=1024, tn=2048):
    t, d = xn.shape
    n = w_in.shape[2]
    return pl.pallas_call(
        _matmul_kernel,
        grid=(t // tm, n // tn),
        in_specs=[pl.BlockSpec((tm, d), lambda i, j: (i, 0)),
                  pl.BlockSpec((None, d, tn), lambda i, j: (layer, 0, j))],
        out_specs=pl.BlockSpec((tm, tn), lambda i, j: (i, j)),
        out_shape=jax.ShapeDtypeStruct((t, n), _BF16),
        compiler_params=_params(("parallel", "parallel")),
        name="in_proj",
    )(xn, w_in)


def _t5_bucket_np(dist):
    n = np.maximum(dist, 0)
    max_exact = NUM_BUCKETS // 2
    nf = np.maximum(n, 1).astype(np.float64)
    large = max_exact + (np.log(nf / max_exact) / math.log(MAX_DISTANCE / max_exact)
                         * (NUM_BUCKETS - max_exact)).astype(np.int32)
    large = np.minimum(large, NUM_BUCKETS - 1)
    return np.where(n < max_exact, n, large).astype(np.int32)


def _bias_tables(rel_bias):
    blk = MOBA_BLOCK
    width = 2 * blk
    e = np.arange(width)
    d_own, d_prev = e - (blk - 1), e + 1
    idx = np.stack([_t5_bucket_np(d_own), _t5_bucket_np(d_prev)])
    table_t = rel_bias.astype(_F32).T
    heads = table_t.shape[0]
    vals = jnp.take(table_t, jnp.asarray(idx), axis=1)
    vals = vals - table_t[:, NUM_BUCKETS - 1][:, None, None]
    causal = np.stack([d_own >= 0, np.ones(width, bool)])
    vals = jnp.where(jnp.asarray(causal)[None], vals, -jnp.inf)
    skew = jnp.broadcast_to(vals[:, :, None, :], (heads, 2, blk, width))
    skew = skew.reshape(heads, 2, blk * width)[:, :, :blk * (width - 1)]
    skew = skew.reshape(heads, 2, blk, width - 1)[:, :, :, blk - 1:width - 1]
    skew = skew * _LOG2_E
    return skew[:, 0], skew[:, 1]


def _attn_kernel(q_ref, k_ref, v_ref, own_ref, prev_ref, o_ref,
                 kmean_ref, vt_ref, qbd_ref, mask_ref, m_ref, acc_ref, raw_ref, pend_p_ref, pend_a_ref,
                 *, nb, heads):
    i = pl.program_id(1)
    blk = MOBA_BLOCK
    dh = HEAD_DIM
    pairs = heads // 2
    log2_scale = dh ** -0.5 * _LOG2_E

    @pl.when(i == 0)
    def _():
        @pl.loop(0, nb)
        def _(n):
            rows = pl.ds(pl.multiple_of(n * blk, blk), blk)
            for h in range(heads):
                cols = slice(h * dh, (h + 1) * dh)
                kf = k_ref[rows, cols].astype(_F32)
                kmean_ref[h, pl.ds(n, 1), :] = jnp.sum(kf, axis=0, keepdims=True) * (1.0 / blk)
                vt_ref[h, n, 0:dh, :] = v_ref[rows, cols].T
                vt_ref[h, n, dh:, :] = jnp.ones((vt_ref.shape[2] - dh, blk), vt_ref.dtype)

    blk_id = lax.broadcasted_iota(jnp.int32, (nb, blk), 0)
    past = blk_id < i
    zeros = jnp.zeros((dh, blk), qbd_ref.dtype)
    for h in range(heads):
        g, half = divmod(h, 2)
        lanes = slice(half * blk, (half + 1) * blk)
        qf = q_ref[:, h * dh:(h + 1) * dh].T.astype(_F32)
        qs = (qf * log2_scale).astype(qbd_ref.dtype)
        qp = qf.astype(qbd_ref.dtype)
        for v, q in enumerate((qp, qs)):
            qbd_ref[v, g, half * dh:(half + 1) * dh, :] = (
                jnp.concatenate([q, zeros] if half == 0 else [zeros, q], axis=1))
        gate = jnp.dot(kmean_ref[h], qf, precision=lax.Precision.HIGHEST,
                       preferred_element_type=_F32)
        gate = jnp.where(past, gate, -jnp.inf)
        rank = jnp.zeros((nb, blk), jnp.int32)
        for n in range(nb):
            row = gate[n:n + 1, :]
            beats = (row > gate) | ((row == gate) & (n < blk_id))
            rank = rank + beats.astype(jnp.int32)
        mask_ref[g, :, lanes] = jnp.where((rank < MOBA_TOPK) & past, 0.0, -jnp.inf)

    ahead = raw_ref.shape[0]
    defer = pend_p_ref.shape[0]

    def qk(j, g, scaled):
        start = j * blk if isinstance(j, int) else pl.multiple_of(j * blk, blk)
        return jnp.dot(k_ref[pl.ds(start, blk), g * 2 * dh:(g + 1) * 2 * dh],
                       qbd_ref[int(scaled), g], preferred_element_type=_F32)

    def accumulate(g, j, p, alpha):
        for half in range(2):
            h = 2 * g + half
            lanes = slice(half * blk, (half + 1) * blk)
            pv = jnp.dot(vt_ref[h, j], p[:, lanes], preferred_element_type=_F32)
            acc_ref[h] = pv if alpha is None else alpha[:, lanes] * acc_ref[h] + pv

    def flush(pend_j, pend_first):
        for d in range(defer):
            accumulate(pairs - defer + d, pend_j, pend_p_ref[d],
                       None if pend_first else pend_a_ref[d])

    def stage(j, scores_of_pair, first, raw, next_j, pending, scaled, next_scaled):
        if pending is not None:
            flush(*pending)
        raw = [qk(j, g, scaled) for g in range(ahead)] if raw is None else list(raw)
        raw_next = []
        for g in range(pairs):
            if g + ahead < pairs:
                raw.append(qk(j, g + ahead, scaled))
            else:
                raw_next.append(qk(next_j, g + ahead - pairs, next_scaled))
            s = scores_of_pair(g, raw[g])
            m_new = jnp.max(s, axis=0, keepdims=True)
            alpha = None
            if not first:
                m_old = m_ref[g]
                m_new = jnp.maximum(m_old, m_new)
                alpha = jnp.exp2(m_old - m_new)
            m_ref[g] = m_new
            p = jnp.exp2(s - m_new).astype(vt_ref.dtype)
            if g >= pairs - defer:
                pend_p_ref[g - (pairs - defer)] = p
                if not first:
                    pend_a_ref[g - (pairs - defer)] = alpha
            else:
                accumulate(g, j, p, alpha)
        return raw_next

    jp = jnp.maximum(i - 1, 0)
    raw = stage(i, lambda g, r: r * log2_scale + own_ref[g], True, None, jp, None, False, False)
    raw = stage(jp, lambda g, r: r * log2_scale + (prev_ref[g] + mask_ref[g, pl.ds(jp, 1), :]), False,
                raw, 0, (i, True), False, True)
    for a in range(ahead):
        raw_ref[a] = raw[a]

    @pl.loop(0, jp)
    def _(j):
        raw = stage(j, lambda g, r: jnp.where(mask_ref[g, pl.ds(j, 1), :] == 0.0, r, -jnp.inf), False,
                    [raw_ref[a] for a in range(ahead)], j + 1,
                    (jnp.where(j == 0, jp, j - 1), False), True, True)
        for a in range(ahead):
            raw_ref[a] = raw[a]

    flush(jnp.maximum(jp - 1, 0), False)
    for h in range(heads):
        acc = acc_ref[h]
        out = acc[0:dh, :] * (1.0 / acc[dh:dh + 1, :])
        o_ref[:, h * dh:(h + 1) * dh] = out.T.astype(o_ref.dtype)


def _moba_attention(proj, bias_own, bias_prev, batch, seq, q_col):
    t = proj.shape[0]
    nb = seq // MOBA_BLOCK
    blk = MOBA_BLOCK
    heads = N_HEADS
    pairs = heads // 2
    w = heads * HEAD_DIM
    q_blk = q_col // w
    ones_rows = 16
    once = pl.Buffered(1)

    def pair_layout(bias):
        return bias.reshape(pairs, 2, blk, blk).transpose(0, 2, 1, 3).reshape(pairs, blk, 2 * blk)

    kernel = functools.partial(_attn_kernel, nb=nb, heads=heads)
    return pl.pallas_call(
        kernel,
        grid=(batch, nb),
        in_specs=[
            pl.BlockSpec((blk, w), lambda b, i: (b * nb + i, q_blk)),
            pl.BlockSpec((seq, w), lambda b, i: (b, q_blk + 1)),
            pl.BlockSpec((seq, w), lambda b, i: (b, q_blk + 2)),
            pl.BlockSpec((pairs, blk, 2 * blk), lambda b, i: (0, 0, 0), pipeline_mode=once),
            pl.BlockSpec((pairs, blk, 2 * blk), lambda b, i: (0, 0, 0), pipeline_mode=once),
        ],
        out_specs=pl.BlockSpec((blk, w), lambda b, i: (b * nb + i, 0)),
        out_shape=jax.ShapeDtypeStruct((t, w), _BF16),
        scratch_shapes=[
            pltpu.VMEM((heads, nb, HEAD_DIM), _F32),
            pltpu.VMEM((heads, nb, HEAD_DIM + ones_rows, blk), _BF16),
            pltpu.VMEM((2, pairs, 2 * HEAD_DIM, 2 * blk), _BF16),
            pltpu.VMEM((pairs, nb, 2 * blk), _F32),
            pltpu.VMEM((pairs, 1, 2 * blk), _F32),
            pltpu.VMEM((heads, HEAD_DIM + ones_rows, blk), _F32),
            pltpu.VMEM((pairs // 2, blk, 2 * blk), _F32),
            pltpu.VMEM((1, blk, 2 * blk), _BF16),
            pltpu.VMEM((1, 1, 2 * blk), _F32),
        ],
        compiler_params=_params(("parallel", "arbitrary")),
        name="moba_attention",
    )(proj, proj, proj, pair_layout(bias_own), pair_layout(bias_prev))


def _mixer_kernel(hin_ref, gb_ref, gc_ref, hin_halo_ref, gc_halo_ref, gconv_ref, gatt_ref,
                  att_ref, x_ref, cw_ref, wco_ref, wao_ref, wmix_ref, g_ref,
                  h_ref, hn_ref, *, tiles_per_seq):
    i = pl.program_id(0)
    tm = hin_ref.shape[0]
    halo = hin_halo_ref.shape[0]

    u = gc_ref[...].astype(_F32) * hin_ref[...].astype(_F32)
    keep = (i % tiles_per_seq != 0).astype(_F32)
    uh = gc_halo_ref[...].astype(_F32) * hin_halo_ref[...].astype(_F32) * keep
    prev1 = uh[halo - 1:halo, :]
    prev2 = uh[halo - 2:halo - 1, :]
    row = lax.broadcasted_iota(jnp.int32, u.shape, 0)
    u1 = jnp.where(row == 0, prev1, pltpu.roll(u, 1, 0))
    u2 = jnp.where(row == 0, prev2, jnp.where(row == 1, prev1, pltpu.roll(u, 2, 0)))
    cw = cw_ref[...]
    y = cw[0:1, :] * u2 + cw[1:2, :] * u1 + cw[2:3, :] * u
    yc = (gb_ref[...].astype(_F32) * y).astype(_BF16)

    y_conv = jnp.dot(yc, wco_ref[...], preferred_element_type=_F32)
    y_att = jnp.dot(att_ref[...], wao_ref[...], preferred_element_type=_F32)
    merged = (jax.nn.sigmoid(gconv_ref[...].astype(_F32)) * y_conv
              + jax.nn.sigmoid(gatt_ref[...].astype(_F32)) * y_att)
    h = x_ref[...] + jnp.dot(merged.astype(_BF16), wmix_ref[...], preferred_element_type=_F32)
    h_ref[...] = h
    hn_ref[...] = ((h * _rms_scale(h)) * g_ref[...]).astype(hn_ref.dtype)


def _mixer_out(proj, att, x, conv_w, w_co, w_ao, w_mix, g_ffn, layer, seq, tm=256, halo=16):
    t, d = x.shape
    c = conv_w.shape[2]
    aw = att.shape[1]
    assert seq % tm == 0 and tm % halo == 0
    gate_col = (3 * c + 3 * aw) // d
    resident = pl.Buffered(1)
    kernel = functools.partial(_mixer_kernel, tiles_per_seq=seq // tm)
    halo_idx = lambda i: jnp.maximum(i * (tm // halo) - 1, 0)
    return pl.pallas_call(
        kernel,
        grid=(t // tm,),
        in_specs=[
            pl.BlockSpec((tm, c), lambda i: (i, 0)),
            pl.BlockSpec((tm, c), lambda i: (i, 1)),
            pl.BlockSpec((tm, c), lambda i: (i, 2)),
            pl.BlockSpec((halo, c), lambda i: (halo_idx(i), 0)),
            pl.BlockSpec((halo, c), lambda i: (halo_idx(i), 2)),
            pl.BlockSpec((tm, d), lambda i: (i, gate_col)),
            pl.BlockSpec((tm, d), lambda i: (i, gate_col + 1)),
            pl.BlockSpec((tm, aw), lambda i: (i, 0)),
            pl.BlockSpec((tm, d), lambda i: (i, 0)),
            pl.BlockSpec((None, CONV_K, c), lambda i: (layer, 0, 0)),
            pl.BlockSpec((None, c, d), lambda i: (layer, 0, 0), pipeline_mode=resident),
            pl.BlockSpec((None, aw, d), lambda i: (layer, 0, 0), pipeline_mode=resident),
            pl.BlockSpec((None, d, d), lambda i: (layer, 0, 0), pipeline_mode=resident),
            pl.BlockSpec((1, d), lambda i: (0, 0)),
        ],
        out_specs=[pl.BlockSpec((tm, d), lambda i: (i, 0)),
                   pl.BlockSpec((tm, d), lambda i: (i, 0))],
        out_shape=[jax.ShapeDtypeStruct((t, d), _F32),
                   jax.ShapeDtypeStruct((t, d), _BF16)],
        compiler_params=_params(("parallel",)),
        name="mixer_out",
    )(proj, proj, proj, proj, proj, proj, proj, att, x, conv_w, w_co, w_ao, w_mix, g_ffn)


def _ffn_act_kernel(hn_ref, wg_ref, wu_ref, o_ref, wg_bf16_ref, wu_bf16_ref, *, row_chunk):
    @pl.when(pl.program_id(1) == 0)
    def _():
        wg_bf16_ref[...] = wg_ref[...].astype(wg_bf16_ref.dtype)
        wu_bf16_ref[...] = wu_ref[...].astype(wu_bf16_ref.dtype)

    for r0 in range(0, hn_ref.shape[0], row_chunk):
        rows = slice(r0, r0 + row_chunk)
        hn = hn_ref[rows, :]
        a = jnp.dot(hn, wg_bf16_ref[...], preferred_element_type=_F32)
        b = jnp.dot(hn, wu_bf16_ref[...], preferred_element_type=_F32)
        o_ref[rows, :] = ((a * jax.nn.sigmoid(a)) * b).astype(o_ref.dtype)


def _ffn_act(hn, w_gate, w_up, layer, tm=2048, tn=512, row_chunk=256):
    t, d = hn.shape
    ff = w_gate.shape[2]
    w_spec = pl.BlockSpec((None, d, tn), lambda j, i: (layer, 0, j))
    return pl.pallas_call(
        functools.partial(_ffn_act_kernel, row_chunk=row_chunk),
        grid=(ff // tn, t // tm),
        in_specs=[pl.BlockSpec((tm, d), lambda j, i: (i, 0)), w_spec, w_spec],
        out_specs=pl.BlockSpec((tm, tn), lambda j, i: (i, j)),
        out_shape=jax.ShapeDtypeStruct((t, ff), _BF16),
        scratch_shapes=[pltpu.VMEM((d, tn), _BF16), pltpu.VMEM((d, tn), _BF16)],
        compiler_params=_params(("parallel", "arbitrary")),
        name="ffn_act",
    )(hn, w_gate, w_up)


def _ffn_down_kernel(act_ref, wd_ref, h_ref, g_ref, *out_refs):
    x = h_ref[...] + jnp.dot(act_ref[...], wd_ref[...], preferred_element_type=_F32)
    if len(out_refs) == 2:
        out_refs[0][...] = x
    xn_ref = out_refs[-1]
    xn_ref[...] = ((x * _rms_scale(x)) * g_ref[...]).astype(xn_ref.dtype)


def _ffn_down(act, h, w_down, g_next, layer, emit_x, xn_dtype, tm=256):
    t, d = h.shape
    ff = w_down.shape[1]
    row_spec = pl.BlockSpec((tm, d), lambda i: (i, 0))
    out_specs = [row_spec, row_spec] if emit_x else [row_spec]
    out_shape = [jax.ShapeDtypeStruct((t, d), xn_dtype)]
    if emit_x:
        out_shape = [jax.ShapeDtypeStruct((t, d), _F32)] + out_shape
    return pl.pallas_call(
        _ffn_down_kernel,
        grid=(t // tm,),
        in_specs=[
            pl.BlockSpec((tm, ff), lambda i: (i, 0)),
            pl.BlockSpec((None, ff, d), lambda i: (layer, 0, 0), pipeline_mode=pl.Buffered(1)),
            row_spec,
            pl.BlockSpec((1, d), lambda i: (0, 0)),
        ],
        out_specs=out_specs,
        out_shape=out_shape,
        compiler_params=_params(("parallel",)),
        name="ffn_down",
    )(act, w_down, h, g_next)


def kernel(x, w_in, conv_w, w_conv_out, w_attn_out, w_mix_out, rel_bias, norm_mix, norm_ffn,
           w_ffn_gate, w_ffn_up, w_ffn_down, norm_final):
    batch, seq, d = x.shape
    depth = w_in.shape[0]
    c = conv_w.shape[2]
    aw = N_HEADS * HEAD_DIM
    t = batch * seq
    assert seq % MOBA_BLOCK == 0 and seq // MOBA_BLOCK >= MOBA_TOPK
    assert w_in.shape[2] == 3 * c + 3 * aw + 2 * d

    w_in, w_co, w_ao, w_mix = (w.astype(_BF16) for w in (w_in, w_conv_out, w_attn_out, w_mix_out))
    w_down = w_ffn_down.astype(_BF16)
    bias_own, bias_prev = _bias_tables(rel_bias)
    q_col = 3 * c

    x = x.reshape(t, d)
    xn = _rmsnorm(x, norm_mix[0:1])
    for layer in range(depth):
        proj = _in_proj(xn, w_in, layer)
        att = _moba_attention(proj, bias_own, bias_prev, batch, seq, q_col)
        h, hn = _mixer_out(proj, att, x, conv_w, w_co, w_ao, w_mix, norm_ffn[layer:layer + 1],
                           layer, seq)
        act = _ffn_act(hn, w_ffn_gate, w_ffn_up, layer)
        if layer + 1 < depth:
            x, xn = _ffn_down(act, h, w_down, norm_mix[layer + 1:layer + 2], layer,
                              emit_x=True, xn_dtype=_BF16)
        else:
            (out,) = _ffn_down(act, h, w_down, norm_final.reshape(1, d), layer,
                               emit_x=False, xn_dtype=_F32)
    return out.reshape(batch, seq, d)
```

```python
import functools
import math

import numpy as np
import jax
import jax.numpy as jnp
from jax import lax
from jax.experimental import pallas as pl
from jax.experimental.pallas import tpu as pltpu

CONV_K = 3
N_HEADS = 8
HEAD_DIM = 128
MOBA_BLOCK = 256
MOBA_TOPK = 3
NUM_BUCKETS = 32
MAX_DISTANCE = 128
EPS = 1e-6
_LOG2_E = math.log2(math.e)

_BF16 = jnp.bfloat16
_F32 = jnp.float32
_V7X_VMEM_LIMIT_BYTES = 56 * 1024 * 1024


def _params(semantics):
    return pltpu.CompilerParams(dimension_semantics=semantics,
                                vmem_limit_bytes=_V7X_VMEM_LIMIT_BYTES)


def _rms_scale(x):
    return lax.rsqrt(jnp.mean(x * x, axis=-1, keepdims=True) + EPS)


def _matmul_kernel(x_ref, w_ref, o_ref):
    o_ref[...] = jnp.dot(x_ref[...], w_ref[...],
                         preferred_element_type=_F32).astype(o_ref.dtype)


def _norm_matmul_kernel(x_ref, g_ref, w_ref, o_ref, xn_ref):
    @pl.when(pl.program_id(1) == 0)
    def _():
        x = x_ref[...]
        xn_ref[...] = ((x * _rms_scale(x)) * g_ref[...]).astype(xn_ref.dtype)

    o_ref[...] = jnp.dot(xn_ref[...], w_ref[...],
                         preferred_element_type=_F32).astype(o_ref.dtype)


def _in_proj(xn, w_in, layer, tm=1024, tn=2048, norm_gain=None):
    t, d = xn.shape
    n = w_in.shape[2]
    x_spec = pl.BlockSpec((tm, d), lambda i, j: (i, 0))
    w_spec = pl.BlockSpec((None, d, tn), lambda i, j: (layer, 0, j))
    common = dict(
        grid=(t // tm, n // tn),
        out_specs=pl.BlockSpec((tm, tn), lambda i, j: (i, j)),
        out_shape=jax.ShapeDtypeStruct((t, n), _BF16),
        name="in_proj",
    )
    if norm_gain is None:
        return pl.pallas_call(_matmul_kernel, in_specs=[x_spec, w_spec],
                              compiler_params=_params(("parallel", "parallel")), **common)(xn, w_in)
    return pl.pallas_call(
        _norm_matmul_kernel,
        in_specs=[x_spec, pl.BlockSpec((1, d), lambda i, j: (0, 0)), w_spec],
        scratch_shapes=[pltpu.VMEM((tm, d), _BF16)],
        compiler_params=_params(("parallel", "arbitrary")), **common)(xn, norm_gain, w_in)


def _t5_bucket_np(dist):
    n = np.maximum(dist, 0)
    max_exact = NUM_BUCKETS // 2
    nf = np.maximum(n, 1).astype(np.float64)
    large = max_exact + (np.log(nf / max_exact) / math.log(MAX_DISTANCE / max_exact)
                         * (NUM_BUCKETS - max_exact)).astype(np.int32)
    large = np.minimum(large, NUM_BUCKETS - 1)
    return np.where(n < max_exact, n, large).astype(np.int32)


def _bias_tables(rel_bias):
    blk = MOBA_BLOCK
    width = 2 * blk
    e = np.arange(width)
    d_own, d_prev = e - (blk - 1), e + 1
    idx = np.stack([_t5_bucket_np(d_own), _t5_bucket_np(d_prev)])
    table_t = rel_bias.astype(_F32).T
    heads = table_t.shape[0]
    vals = jnp.take(table_t, jnp.asarray(idx), axis=1)
    vals = vals - table_t[:, NUM_BUCKETS - 1][:, None, None]
    causal = np.stack([d_own >= 0, np.ones(width, bool)])
    vals = jnp.where(jnp.asarray(causal)[None], vals, -jnp.inf)
    skew = jnp.broadcast_to(vals[:, :, None, :], (heads, 2, blk, width))
    skew = skew.reshape(heads, 2, blk * width)[:, :, :blk * (width - 1)]
    skew = skew.reshape(heads, 2, blk, width - 1)[:, :, :, blk - 1:width - 1]
    skew = skew * _LOG2_E
    return skew[:, 0], skew[:, 1]


def _attn_kernel(q_ref, k_ref, v_ref, own_ref, prev_ref, o_ref,
                 kmean_ref, vt_ref, qbd_ref, mask_ref, m_ref, acc_ref, raw_ref, pend_p_ref, pend_a_ref,
                 *, nb, heads):
    i = pl.program_id(1)
    blk = MOBA_BLOCK
    dh = HEAD_DIM
    pairs = heads // 2
    log2_scale = dh ** -0.5 * _LOG2_E

    @pl.when(i == 0)
    def _():
        @pl.loop(0, nb)
        def _(n):
            rows = pl.ds(pl.multiple_of(n * blk, blk), blk)
            for h in range(heads):
                cols = slice(h * dh, (h + 1) * dh)
                kf = k_ref[rows, cols].astype(_F32)
                kmean_ref[h, pl.ds(n, 1), :] = jnp.sum(kf, axis=0, keepdims=True) * (1.0 / blk)
                vt_ref[h, n, 0:dh, :] = v_ref[rows, cols].T
                vt_ref[h, n, dh:, :] = jnp.ones((vt_ref.shape[2] - dh, blk), vt_ref.dtype)

    blk_id = lax.broadcasted_iota(jnp.int32, (nb, blk), 0)
    past = blk_id < i
    zeros = jnp.zeros((dh, blk), qbd_ref.dtype)
    for h in range(heads):
        g, half = divmod(h, 2)
        lanes = slice(half * blk, (half + 1) * blk)
        qf = q_ref[:, h * dh:(h + 1) * dh].T.astype(_F32)
        qs = (qf * log2_scale).astype(qbd_ref.dtype)
        qp = qf.astype(qbd_ref.dtype)
        for v, q in enumerate((qp, qs)):
            qbd_ref[v, g, half * dh:(half + 1) * dh, :] = (
                jnp.concatenate([q, zeros] if half == 0 else [zeros, q], axis=1))
        gate = jnp.dot(kmean_ref[h], qf, precision=lax.Precision.HIGHEST,
                       preferred_element_type=_F32)
        gate = jnp.where(past, gate, -jnp.inf)
        rank = jnp.zeros((nb, blk), jnp.int32)
        for n in range(nb):
            row = gate[n:n + 1, :]
            beats = (row > gate) | ((row == gate) & (n < blk_id))
            rank = rank + beats.astype(jnp.int32)
        mask_ref[g, :, lanes] = jnp.where((rank < MOBA_TOPK) & past, 0.0, -jnp.inf)

    ahead = raw_ref.shape[0]
    defer = pend_p_ref.shape[0]

    def qk(j, g, scaled):
        start = j * blk if isinstance(j, int) else pl.multiple_of(j * blk, blk)
        return jnp.dot(k_ref[pl.ds(start, blk), g * 2 * dh:(g + 1) * 2 * dh],
                       qbd_ref[int(scaled), g], preferred_element_type=_F32)

    def accumulate(g, j, p, alpha):
        for half in range(2):
            h = 2 * g + half
            lanes = slice(half * blk, (half + 1) * blk)
            pv = jnp.dot(vt_ref[h, j], p[:, lanes], preferred_element_type=_F32)
            acc_ref[h] = pv if alpha is None else alpha[:, lanes] * acc_ref[h] + pv

    def flush(pend_j, pend_first):
        for d in range(defer):
            accumulate(pairs - defer + d, pend_j, pend_p_ref[d],
                       None if pend_first else pend_a_ref[d])

    def stage(j, scores_of_pair, first, raw, next_j, pending, scaled, next_scaled):
        if pending is not None:
            flush(*pending)
        raw = [qk(j, g, scaled) for g in range(ahead)] if raw is None else list(raw)
        raw_next = []
        for g in range(pairs):
            if g + ahead < pairs:
                raw.append(qk(j, g + ahead, scaled))
            else:
                raw_next.append(qk(next_j, g + ahead - pairs, next_scaled))
            s = scores_of_pair(g, raw[g])
            m_new = jnp.max(s, axis=0, keepdims=True)
            alpha = None
            if not first:
                m_old = m_ref[g]
                m_new = jnp.maximum(m_old, m_new)
                alpha = jnp.exp2(m_old - m_new)
            m_ref[g] = m_new
            p = jnp.exp2(s - m_new).astype(vt_ref.dtype)
            if g >= pairs - defer:
                pend_p_ref[g - (pairs - defer)] = p
                if not first:
                    pend_a_ref[g - (pairs - defer)] = alpha
            else:
                accumulate(g, j, p, alpha)
        return raw_next

    jp = jnp.maximum(i - 1, 0)
    raw = stage(i, lambda g, r: r * log2_scale + own_ref[g], True, None, jp, None, False, False)
    raw = stage(jp, lambda g, r: r * log2_scale + (prev_ref[g] + mask_ref[g, pl.ds(jp, 1), :]), False,
                raw, 0, (i, True), False, True)
    for a in range(ahead):
        raw_ref[a] = raw[a]

    @pl.loop(0, jp)
    def _(j):
        raw = stage(j, lambda g, r: jnp.where(mask_ref[g, pl.ds(j, 1), :] == 0.0, r, -jnp.inf), False,
                    [raw_ref[a] for a in range(ahead)], j + 1,
                    (jnp.where(j == 0, jp, j - 1), False), True, True)
        for a in range(ahead):
            raw_ref[a] = raw[a]

    flush(jnp.maximum(jp - 1, 0), False)
    for h in range(heads):
        acc = acc_ref[h]
        out = acc[0:dh, :] * (1.0 / acc[dh:dh + 1, :])
        o_ref[:, h * dh:(h + 1) * dh] = out.T.astype(o_ref.dtype)


def _moba_attention(proj, bias_own, bias_prev, batch, seq, q_col):
    t = proj.shape[0]
    nb = seq // MOBA_BLOCK
    blk = MOBA_BLOCK
    heads = N_HEADS
    pairs = heads // 2
    w = heads * HEAD_DIM
    q_blk = q_col // w
    ones_rows = 16
    once = pl.Buffered(1)

    def pair_layout(bias):
        return bias.reshape(pairs, 2, blk, blk).transpose(0, 2, 1, 3).reshape(pairs, blk, 2 * blk)

    kernel = functools.partial(_attn_kernel, nb=nb, heads=heads)
    return pl.pallas_call(
        kernel,
        grid=(batch, nb),
        in_specs=[
            pl.BlockSpec((blk, w), lambda b, i: (b * nb + i, q_blk)),
            pl.BlockSpec((seq, w), lambda b, i: (b, q_blk + 1)),
            pl.BlockSpec((seq, w), lambda b, i: (b, q_blk + 2)),
            pl.BlockSpec((pairs, blk, 2 * blk), lambda b, i: (0, 0, 0), pipeline_mode=once),
            pl.BlockSpec((pairs, blk, 2 * blk), lambda b, i: (0, 0, 0), pipeline_mode=once),
        ],
        out_specs=pl.BlockSpec((blk, w), lambda b, i: (b * nb + i, 0)),
        out_shape=jax.ShapeDtypeStruct((t, w), _BF16),
        scratch_shapes=[
            pltpu.VMEM((heads, nb, HEAD_DIM), _F32),
            pltpu.VMEM((heads, nb, HEAD_DIM + ones_rows, blk), _BF16),
            pltpu.VMEM((2, pairs, 2 * HEAD_DIM, 2 * blk), _BF16),
            pltpu.VMEM((pairs, nb, 2 * blk), _F32),
            pltpu.VMEM((pairs, 1, 2 * blk), _F32),
            pltpu.VMEM((heads, HEAD_DIM + ones_rows, blk), _F32),
            pltpu.VMEM((pairs // 2, blk, 2 * blk), _F32),
            pltpu.VMEM((1, blk, 2 * blk), _BF16),
            pltpu.VMEM((1, 1, 2 * blk), _F32),
        ],
        compiler_params=_params(("parallel", "arbitrary")),
        name="moba_attention",
    )(proj, proj, proj, pair_layout(bias_own), pair_layout(bias_prev))


def _mixer_kernel(hin_ref, gb_ref, gc_ref, hin_halo_ref, gc_halo_ref, gconv_ref, gatt_ref,
                  att_ref, x_ref, cw_ref, wco_ref, wao_ref, wmix_ref, g_ref,
                  h_ref, hn_ref, *, tiles_per_seq, conv_chunk):
    i = pl.program_id(0)
    tm = hin_ref.shape[0]
    halo = hin_halo_ref.shape[0]

    keep = (i % tiles_per_seq != 0).astype(_F32)
    cw = cw_ref[...]
    row = lax.broadcasted_iota(jnp.int32, (tm, conv_chunk), 0)
    y_conv = None
    for c0 in range(0, hin_ref.shape[1], conv_chunk):
        ch = slice(c0, c0 + conv_chunk)
        u = gc_ref[:, ch].astype(_F32) * hin_ref[:, ch].astype(_F32)
        uh = gc_halo_ref[:, ch].astype(_F32) * hin_halo_ref[:, ch].astype(_F32) * keep
        prev1 = uh[halo - 1:halo, :]
        prev2 = uh[halo - 2:halo - 1, :]
        u1 = jnp.where(row == 0, prev1, pltpu.roll(u, 1, 0))
        u2 = jnp.where(row == 0, prev2, jnp.where(row == 1, prev1, pltpu.roll(u, 2, 0)))
        y = cw[0:1, ch] * u2 + cw[1:2, ch] * u1 + cw[2:3, ch] * u
        yc = (gb_ref[:, ch].astype(_F32) * y).astype(_BF16)
        part = jnp.dot(yc, wco_ref[ch, :], preferred_element_type=_F32)
        y_conv = part if y_conv is None else y_conv + part
    y_att = jnp.dot(att_ref[...], wao_ref[...], preferred_element_type=_F32)
    merged = (jax.nn.sigmoid(gconv_ref[...].astype(_F32)) * y_conv
              + jax.nn.sigmoid(gatt_ref[...].astype(_F32)) * y_att)
    h = x_ref[...] + jnp.dot(merged.astype(_BF16), wmix_ref[...], preferred_element_type=_F32)
    h_ref[...] = h
    hn_ref[...] = ((h * _rms_scale(h)) * g_ref[...]).astype(hn_ref.dtype)


def _mixer_out(proj, att, x, conv_w, w_co, w_ao, w_mix, g_ffn, layer, seq, tm=256, halo=16,
               conv_chunk=256):
    t, d = x.shape
    c = conv_w.shape[2]
    aw = att.shape[1]
    assert seq % tm == 0 and tm % halo == 0
    gate_col = (3 * c + 3 * aw) // d
    resident = pl.Buffered(1)
    kernel = functools.partial(_mixer_kernel, tiles_per_seq=seq // tm, conv_chunk=conv_chunk)
    halo_idx = lambda i: jnp.maximum(i * (tm // halo) - 1, 0)
    return pl.pallas_call(
        kernel,
        grid=(t // tm,),
        in_specs=[
            pl.BlockSpec((tm, c), lambda i: (i, 0)),
            pl.BlockSpec((tm, c), lambda i: (i, 1)),
            pl.BlockSpec((tm, c), lambda i: (i, 2)),
            pl.BlockSpec((halo, c), lambda i: (halo_idx(i), 0)),
            pl.BlockSpec((halo, c), lambda i: (halo_idx(i), 2)),
            pl.BlockSpec((tm, d), lambda i: (i, gate_col)),
            pl.BlockSpec((tm, d), lambda i: (i, gate_col + 1)),
            pl.BlockSpec((tm, aw), lambda i: (i, 0)),
            pl.BlockSpec((tm, d), lambda i: (i, 0)),
            pl.BlockSpec((None, CONV_K, c), lambda i: (layer, 0, 0)),
            pl.BlockSpec((None, c, d), lambda i: (layer, 0, 0), pipeline_mode=resident),
            pl.BlockSpec((None, aw, d), lambda i: (layer, 0, 0), pipeline_mode=resident),
            pl.BlockSpec((None, d, d), lambda i: (layer, 0, 0), pipeline_mode=resident),
            pl.BlockSpec((1, d), lambda i: (0, 0)),
        ],
        out_specs=[pl.BlockSpec((tm, d), lambda i: (i, 0)),
                   pl.BlockSpec((tm, d), lambda i: (i, 0))],
        out_shape=[jax.ShapeDtypeStruct((t, d), _F32),
                   jax.ShapeDtypeStruct((t, d), _BF16)],
        compiler_params=_params(("parallel",)),
        name="mixer_out",
    )(proj, proj, proj, proj, proj, proj, proj, att, x, conv_w, w_co, w_ao, w_mix, g_ffn)


def _ffn_act_kernel(hn_ref, wg_ref, wu_ref, o_ref, wg_bf16_ref, wu_bf16_ref, *, row_chunk):
    @pl.when(pl.program_id(1) == 0)
    def _():
        wg_bf16_ref[...] = wg_ref[...].astype(wg_bf16_ref.dtype)
        wu_bf16_ref[...] = wu_ref[...].astype(wu_bf16_ref.dtype)

    for r0 in range(0, hn_ref.shape[0], row_chunk):
        rows = slice(r0, r0 + row_chunk)
        hn = hn_ref[rows, :]
        a = jnp.dot(hn, wg_bf16_ref[...], preferred_element_type=_F32)
        b = jnp.dot(hn, wu_bf16_ref[...], preferred_element_type=_F32)
        o_ref[rows, :] = ((a * jax.nn.sigmoid(a)) * b).astype(o_ref.dtype)


def _ffn_act(hn, w_gate, w_up, layer, tm=2048, tn=512, row_chunk=256):
    t, d = hn.shape
    ff = w_gate.shape[2]
    w_spec = pl.BlockSpec((None, d, tn), lambda j, i: (layer, 0, j))
    return pl.pallas_call(
        functools.partial(_ffn_act_kernel, row_chunk=row_chunk),
        grid=(ff // tn, t // tm),
        in_specs=[pl.BlockSpec((tm, d), lambda j, i: (i, 0)), w_spec, w_spec],
        out_specs=pl.BlockSpec((tm, tn), lambda j, i: (i, j)),
        out_shape=jax.ShapeDtypeStruct((t, ff), _BF16),
        scratch_shapes=[pltpu.VMEM((d, tn), _BF16), pltpu.VMEM((d, tn), _BF16)],
        compiler_params=_params(("parallel", "arbitrary")),
        name="ffn_act",
    )(hn, w_gate, w_up)


def _ffn_down_kernel(act_ref, wd_ref, h_ref, g_ref, *out_refs):
    x = h_ref[...] + jnp.dot(act_ref[...], wd_ref[...], preferred_element_type=_F32)
    if len(out_refs) == 2:
        out_refs[0][...] = x
    xn_ref = out_refs[-1]
    xn_ref[...] = ((x * _rms_scale(x)) * g_ref[...]).astype(xn_ref.dtype)


def _ffn_down(act, h, w_down, g_next, layer, emit_x, xn_dtype, tm=256):
    t, d = h.shape
    ff = w_down.shape[1]
    row_spec = pl.BlockSpec((tm, d), lambda i: (i, 0))
    out_specs = [row_spec, row_spec] if emit_x else [row_spec]
    out_shape = [jax.ShapeDtypeStruct((t, d), xn_dtype)]
    if emit_x:
        out_shape = [jax.ShapeDtypeStruct((t, d), _F32)] + out_shape
    return pl.pallas_call(
        _ffn_down_kernel,
        grid=(t // tm,),
        in_specs=[
            pl.BlockSpec((tm, ff), lambda i: (i, 0)),
            pl.BlockSpec((None, ff, d), lambda i: (layer, 0, 0), pipeline_mode=pl.Buffered(1)),
            row_spec,
            pl.BlockSpec((1, d), lambda i: (0, 0)),
        ],
        out_specs=out_specs,
        out_shape=out_shape,
        compiler_params=_params(("parallel",)),
        name="ffn_down",
    )(act, w_down, h, g_next)


def kernel(x, w_in, conv_w, w_conv_out, w_attn_out, w_mix_out, rel_bias, norm_mix, norm_ffn,
           w_ffn_gate, w_ffn_up, w_ffn_down, norm_final):
    batch, seq, d = x.shape
    depth = w_in.shape[0]
    c = conv_w.shape[2]
    aw = N_HEADS * HEAD_DIM
    t = batch * seq
    assert seq % MOBA_BLOCK == 0 and seq // MOBA_BLOCK >= MOBA_TOPK
    assert w_in.shape[2] == 3 * c + 3 * aw + 2 * d

    w_in, w_co, w_ao, w_mix = (w.astype(_BF16) for w in (w_in, w_conv_out, w_attn_out, w_mix_out))
    w_down = w_ffn_down.astype(_BF16)
    bias_own, bias_prev = _bias_tables(rel_bias)
    q_col = 3 * c

    x = x.reshape(t, d)
    for layer in range(depth):
        if layer == 0:
            proj = _in_proj(x, w_in, layer, norm_gain=norm_mix[0:1])
        else:
            proj = _in_proj(xn, w_in, layer)
        att = _moba_attention(proj, bias_own, bias_prev, batch, seq, q_col)
        h, hn = _mixer_out(proj, att, x, conv_w, w_co, w_ao, w_mix, norm_ffn[layer:layer + 1],
                           layer, seq)
        act = _ffn_act(hn, w_ffn_gate, w_ffn_up, layer)
        if layer + 1 < depth:
            x, xn = _ffn_down(act, h, w_down, norm_mix[layer + 1:layer + 2], layer,
                              emit_x=True, xn_dtype=_BF16)
        else:
            (out,) = _ffn_down(act, h, w_down, norm_final.reshape(1, d), layer,
                               emit_x=False, xn_dtype=_F32)
    return out.reshape(batch, seq, d)
```

```python
import functools
import math

import numpy as np
import jax
import jax.numpy as jnp
from jax import lax
from jax.experimental import pallas as pl
from jax.experimental.pallas import tpu as pltpu

CONV_K = 3
N_HEADS = 8
HEAD_DIM = 128
MOBA_BLOCK = 256
MOBA_TOPK = 3
NUM_BUCKETS = 32
MAX_DISTANCE = 128
EPS = 1e-6
_LOG2_E = math.log2(math.e)

_BF16 = jnp.bfloat16
_F32 = jnp.float32
_V7X_VMEM_LIMIT_BYTES = 56 * 1024 * 1024


def _params(semantics):
    return pltpu.CompilerParams(dimension_semantics=semantics,
                                vmem_limit_bytes=_V7X_VMEM_LIMIT_BYTES)


def _rms_scale(x):
    return lax.rsqrt(jnp.mean(x * x, axis=-1, keepdims=True) + EPS)


def _matmul_kernel(x_ref, w_ref, o_ref):
    o_ref[...] = jnp.dot(x_ref[...], w_ref[...],
                         preferred_element_type=_F32).astype(o_ref.dtype)


def _norm_matmul_kernel(x_ref, g_ref, w_ref, o_ref, xn_ref):
    @pl.when(pl.program_id(1) == 0)
    def _():
        x = x_ref[...]
        xn_ref[...] = ((x * _rms_scale(x)) * g_ref[...]).astype(xn_ref.dtype)

    o_ref[...] = jnp.dot(xn_ref[...], w_ref[...],
                         preferred_element_type=_F32).astype(o_ref.dtype)


def _in_proj(xn, w_in, layer, tm=1024, tn=2048, norm_gain=None):
    t, d = xn.shape
    n = w_in.shape[2]
    x_spec = pl.BlockSpec((tm, d), lambda i, j: (i, 0))
    w_spec = pl.BlockSpec((None, d, tn), lambda i, j: (layer, 0, j))
    common = dict(
        grid=(t // tm, n // tn),
        out_specs=pl.BlockSpec((tm, tn), lambda i, j: (i, j)),
        out_shape=jax.ShapeDtypeStruct((t, n), _BF16),
        name="in_proj",
    )
    if norm_gain is None:
        return pl.pallas_call(_matmul_kernel, in_specs=[x_spec, w_spec],
                              compiler_params=_params(("parallel", "parallel")), **common)(xn, w_in)
    return pl.pallas_call(
        _norm_matmul_kernel,
        in_specs=[x_spec, pl.BlockSpec((1, d), lambda i, j: (0, 0)), w_spec],
        scratch_shapes=[pltpu.VMEM((tm, d), _BF16)],
        compiler_params=_params(("parallel", "arbitrary")), **common)(xn, norm_gain, w_in)


def _t5_bucket_np(dist):
    n = np.maximum(dist, 0)
    max_exact = NUM_BUCKETS // 2
    nf = np.maximum(n, 1).astype(np.float64)
    large = max_exact + (np.log(nf / max_exact) / math.log(MAX_DISTANCE / max_exact)
                         * (NUM_BUCKETS - max_exact)).astype(np.int32)
    large = np.minimum(large, NUM_BUCKETS - 1)
    return np.where(n < max_exact, n, large).astype(np.int32)


def _bias_tables(rel_bias):
    blk = MOBA_BLOCK
    width = 2 * blk
    e = np.arange(width)
    d_own, d_prev = e - (blk - 1), e + 1
    idx = np.stack([_t5_bucket_np(d_own), _t5_bucket_np(d_prev)])
    table_t = rel_bias.astype(_F32).T
    heads = table_t.shape[0]
    vals = jnp.take(table_t, jnp.asarray(idx), axis=1)
    vals = vals - table_t[:, NUM_BUCKETS - 1][:, None, None]
    causal = np.stack([d_own >= 0, np.ones(width, bool)])
    vals = jnp.where(jnp.asarray(causal)[None], vals, -jnp.inf)
    skew = jnp.broadcast_to(vals[:, :, None, :], (heads, 2, blk, width))
    skew = skew.reshape(heads, 2, blk * width)[:, :, :blk * (width - 1)]
    skew = skew.reshape(heads, 2, blk, width - 1)[:, :, :, blk - 1:width - 1]
    skew = skew * _LOG2_E
    return skew[:, 0], skew[:, 1]


def _attn_kernel(q_ref, k_ref, v_ref, own_ref, prev_ref, o_ref,
                 kmean_ref, vt_ref, qbd_ref, mask_ref, m_ref, acc_ref, raw_ref, pend_p_ref, pend_a_ref,
                 *, nb, heads):
    i = pl.program_id(1)
    blk = MOBA_BLOCK
    dh = HEAD_DIM
    pairs = heads // 2
    log2_scale = dh ** -0.5 * _LOG2_E

    @pl.when(i == 0)
    def _():
        @pl.loop(0, nb)
        def _(n):
            rows = pl.ds(pl.multiple_of(n * blk, blk), blk)
            for h in range(heads):
                cols = slice(h * dh, (h + 1) * dh)
                kf = k_ref[rows, cols].astype(_F32)
                kmean_ref[h, pl.ds(n, 1), :] = jnp.sum(kf, axis=0, keepdims=True) * (1.0 / blk)
                vt_ref[h, n, 0:dh, :] = v_ref[rows, cols].T
                vt_ref[h, n, dh:, :] = jnp.ones((vt_ref.shape[2] - dh, blk), vt_ref.dtype)

    blk_id = lax.broadcasted_iota(jnp.int32, (nb, blk), 0)
    past = blk_id < i
    zeros = jnp.zeros((dh, blk), qbd_ref.dtype)
    for h in range(heads):
        g, half = divmod(h, 2)
        lanes = slice(half * blk, (half + 1) * blk)
        qf = q_ref[:, h * dh:(h + 1) * dh].T.astype(_F32)
        qs = (qf * log2_scale).astype(qbd_ref.dtype)
        qp = qf.astype(qbd_ref.dtype)
        for v, q in enumerate((qp, qs)):
            qbd_ref[v, g, half * dh:(half + 1) * dh, :] = (
                jnp.concatenate([q, zeros] if half == 0 else [zeros, q], axis=1))
        gate = jnp.dot(kmean_ref[h], qf, precision=lax.Precision.HIGHEST,
                       preferred_element_type=_F32)
        gate = jnp.where(past, gate, -jnp.inf)
        rank = jnp.zeros((nb, blk), jnp.int32)
        for n in range(nb):
            row = gate[n:n + 1, :]
            beats = (row > gate) | ((row == gate) & (n < blk_id))
            rank = rank + beats.astype(jnp.int32)
        mask_ref[g, :, lanes] = jnp.where((rank < MOBA_TOPK) & past, 0.0, -jnp.inf)

    ahead = raw_ref.shape[0]
    defer = pend_p_ref.shape[0]

    def qk(j, g, scaled):
        start = j * blk if isinstance(j, int) else pl.multiple_of(j * blk, blk)
        return jnp.dot(k_ref[pl.ds(start, blk), g * 2 * dh:(g + 1) * 2 * dh],
                       qbd_ref[int(scaled), g], preferred_element_type=_F32)

    def accumulate(g, j, p, alpha):
        for half in range(2):
            h = 2 * g + half
            lanes = slice(half * blk, (half + 1) * blk)
            pv = jnp.dot(vt_ref[h, j], p[:, lanes], preferred_element_type=_F32)
            acc_ref[h] = pv if alpha is None else alpha[:, lanes] * acc_ref[h] + pv

    def flush(pend_j, pend_first):
        for d in range(defer):
            accumulate(pairs - defer + d, pend_j, pend_p_ref[d],
                       None if pend_first else pend_a_ref[d])

    def stage(j, scores_of_pair, first, raw, next_j, pending, scaled, next_scaled):
        if pending is not None:
            flush(*pending)
        raw = [qk(j, g, scaled) for g in range(ahead)] if raw is None else list(raw)
        raw_next = []
        for g in range(pairs):
            if g + ahead < pairs:
                raw.append(qk(j, g + ahead, scaled))
            else:
                raw_next.append(qk(next_j, g + ahead - pairs, next_scaled))
            s = scores_of_pair(g, raw[g])
            m_new = jnp.max(s, axis=0, keepdims=True)
            alpha = None
            if not first:
                m_old = m_ref[g]
                m_new = jnp.maximum(m_old, m_new)
                alpha = jnp.exp2(m_old - m_new)
            m_ref[g] = m_new
            p = jnp.exp2(s - m_new).astype(vt_ref.dtype)
            if g >= pairs - defer:
                pend_p_ref[g - (pairs - defer)] = p
                if not first:
                    pend_a_ref[g - (pairs - defer)] = alpha
            else:
                accumulate(g, j, p, alpha)
        return raw_next

    jp = jnp.maximum(i - 1, 0)
    raw = stage(i, lambda g, r: r * log2_scale + own_ref[g], True, None, jp, None, False, False)
    raw = stage(jp, lambda g, r: r * log2_scale + (prev_ref[g] + mask_ref[g, pl.ds(jp, 1), :]), False,
                raw, 0, (i, True), False, True)
    for a in range(ahead):
        raw_ref[a] = raw[a]

    @pl.loop(0, jp)
    def _(j):
        raw = stage(j, lambda g, r: jnp.where(mask_ref[g, pl.ds(j, 1), :] == 0.0, r, -jnp.inf), False,
                    [raw_ref[a] for a in range(ahead)], j + 1,
                    (jnp.where(j == 0, jp, j - 1), False), True, True)
        for a in range(ahead):
            raw_ref[a] = raw[a]

    flush(jnp.maximum(jp - 1, 0), False)
    for h in range(heads):
        acc = acc_ref[h]
        out = acc[0:dh, :] * (1.0 / acc[dh:dh + 1, :])
        o_ref[:, h * dh:(h + 1) * dh] = out.T.astype(o_ref.dtype)


def _moba_attention(proj, bias_own, bias_prev, batch, seq, q_col):
    t = proj.shape[0]
    nb = seq // MOBA_BLOCK
    blk = MOBA_BLOCK
    heads = N_HEADS
    pairs = heads // 2
    w = heads * HEAD_DIM
    q_blk = q_col // w
    ones_rows = 16
    once = pl.Buffered(1)

    def pair_layout(bias):
        return bias.reshape(pairs, 2, blk, blk).transpose(0, 2, 1, 3).reshape(pairs, blk, 2 * blk)

    kernel = functools.partial(_attn_kernel, nb=nb, heads=heads)
    return pl.pallas_call(
        kernel,
        grid=(batch, nb),
        in_specs=[
            pl.BlockSpec((blk, w), lambda b, i: (b * nb + i, q_blk)),
            pl.BlockSpec((seq, w), lambda b, i: (b, q_blk + 1)),
            pl.BlockSpec((seq, w), lambda b, i: (b, q_blk + 2)),
            pl.BlockSpec((pairs, blk, 2 * blk), lambda b, i: (0, 0, 0), pipeline_mode=once),
            pl.BlockSpec((pairs, blk, 2 * blk), lambda b, i: (0, 0, 0), pipeline_mode=once),
        ],
        out_specs=pl.BlockSpec((blk, w), lambda b, i: (b * nb + i, 0)),
        out_shape=jax.ShapeDtypeStruct((t, w), _BF16),
        scratch_shapes=[
            pltpu.VMEM((heads, nb, HEAD_DIM), _F32),
            pltpu.VMEM((heads, nb, HEAD_DIM + ones_rows, blk), _BF16),
            pltpu.VMEM((2, pairs, 2 * HEAD_DIM, 2 * blk), _BF16),
            pltpu.VMEM((pairs, nb, 2 * blk), _F32),
            pltpu.VMEM((pairs, 1, 2 * blk), _F32),
            pltpu.VMEM((heads, HEAD_DIM + ones_rows, blk), _F32),
            pltpu.VMEM((pairs // 2, blk, 2 * blk), _F32),
            pltpu.VMEM((1, blk, 2 * blk), _BF16),
            pltpu.VMEM((1, 1, 2 * blk), _F32),
        ],
        compiler_params=_params(("parallel", "arbitrary")),
        name="moba_attention",
    )(proj, proj, proj, pair_layout(bias_own), pair_layout(bias_prev))


def _mixer_kernel(hin_ref, gb_ref, gc_ref, hin_halo_ref, gc_halo_ref, gconv_ref, gatt_ref,
                  att_ref, x_ref, cw_ref, wco_ref, wao_ref, wmix_ref, g_ref,
                  h_ref, hn_ref, *, tiles_per_seq, conv_chunk, row_chunk):
    i = pl.program_id(0)
    halo = hin_halo_ref.shape[0]
    keep = (i % tiles_per_seq != 0).astype(_F32)
    cw = cw_ref[...]
    row = lax.broadcasted_iota(jnp.int32, (row_chunk, conv_chunk), 0)

    for r0 in range(0, hin_ref.shape[0], row_chunk):
        rows = slice(r0, r0 + row_chunk)
        y_conv = None
        for c0 in range(0, hin_ref.shape[1], conv_chunk):
            ch = slice(c0, c0 + conv_chunk)
            u = gc_ref[rows, ch].astype(_F32) * hin_ref[rows, ch].astype(_F32)
            if r0 == 0:
                uh = gc_halo_ref[:, ch].astype(_F32) * hin_halo_ref[:, ch].astype(_F32) * keep
            else:
                before = slice(r0 - halo, r0)
                uh = gc_ref[before, ch].astype(_F32) * hin_ref[before, ch].astype(_F32)
            prev1 = uh[halo - 1:halo, :]
            prev2 = uh[halo - 2:halo - 1, :]
            u1 = jnp.where(row == 0, prev1, pltpu.roll(u, 1, 0))
            u2 = jnp.where(row == 0, prev2, jnp.where(row == 1, prev1, pltpu.roll(u, 2, 0)))
            y = cw[0:1, ch] * u2 + cw[1:2, ch] * u1 + cw[2:3, ch] * u
            yc = (gb_ref[rows, ch].astype(_F32) * y).astype(_BF16)
            part = jnp.dot(yc, wco_ref[ch, :], preferred_element_type=_F32)
            y_conv = part if y_conv is None else y_conv + part
        y_att = jnp.dot(att_ref[rows, :], wao_ref[...], preferred_element_type=_F32)
        merged = (jax.nn.sigmoid(gconv_ref[rows, :].astype(_F32)) * y_conv
                  + jax.nn.sigmoid(gatt_ref[rows, :].astype(_F32)) * y_att)
        h = x_ref[rows, :] + jnp.dot(merged.astype(_BF16), wmix_ref[...],
                                     preferred_element_type=_F32)
        h_ref[rows, :] = h
        hn_ref[rows, :] = ((h * _rms_scale(h)) * g_ref[...]).astype(hn_ref.dtype)


def _mixer_out(proj, att, x, conv_w, w_co, w_ao, w_mix, g_ffn, layer, seq, tm=512, halo=16,
               conv_chunk=256, row_chunk=256):
    t, d = x.shape
    c = conv_w.shape[2]
    aw = att.shape[1]
    assert seq % tm == 0 and tm % halo == 0
    gate_col = (3 * c + 3 * aw) // d
    resident = pl.Buffered(1)
    kernel = functools.partial(_mixer_kernel, tiles_per_seq=seq // tm, conv_chunk=conv_chunk,
                               row_chunk=row_chunk)
    halo_idx = lambda i: jnp.maximum(i * (tm // halo) - 1, 0)
    return pl.pallas_call(
        kernel,
        grid=(t // tm,),
        in_specs=[
            pl.BlockSpec((tm, c), lambda i: (i, 0)),
            pl.BlockSpec((tm, c), lambda i: (i, 1)),
            pl.BlockSpec((tm, c), lambda i: (i, 2)),
            pl.BlockSpec((halo, c), lambda i: (halo_idx(i), 0)),
            pl.BlockSpec((halo, c), lambda i: (halo_idx(i), 2)),
            pl.BlockSpec((tm, d), lambda i: (i, gate_col)),
            pl.BlockSpec((tm, d), lambda i: (i, gate_col + 1)),
            pl.BlockSpec((tm, aw), lambda i: (i, 0)),
            pl.BlockSpec((tm, d), lambda i: (i, 0)),
            pl.BlockSpec((None, CONV_K, c), lambda i: (layer, 0, 0)),
            pl.BlockSpec((None, c, d), lambda i: (layer, 0, 0), pipeline_mode=resident),
            pl.BlockSpec((None, aw, d), lambda i: (layer, 0, 0), pipeline_mode=resident),
            pl.BlockSpec((None, d, d), lambda i: (layer, 0, 0), pipeline_mode=resident),
            pl.BlockSpec((1, d), lambda i: (0, 0)),
        ],
        out_specs=[pl.BlockSpec((tm, d), lambda i: (i, 0)),
                   pl.BlockSpec((tm, d), lambda i: (i, 0))],
        out_shape=[jax.ShapeDtypeStruct((t, d), _F32),
                   jax.ShapeDtypeStruct((t, d), _BF16)],
        compiler_params=_params(("parallel",)),
        name="mixer_out",
    )(proj, proj, proj, proj, proj, proj, proj, att, x, conv_w, w_co, w_ao, w_mix, g_ffn)


def _ffn_act_kernel(hn_ref, wg_ref, wu_ref, o_ref, wg_bf16_ref, wu_bf16_ref, *, row_chunk):
    @pl.when(pl.program_id(1) == 0)
    def _():
        wg_bf16_ref[...] = wg_ref[...].astype(wg_bf16_ref.dtype)
        wu_bf16_ref[...] = wu_ref[...].astype(wu_bf16_ref.dtype)

    for r0 in range(0, hn_ref.shape[0], row_chunk):
        rows = slice(r0, r0 + row_chunk)
        hn = hn_ref[rows, :]
        a = jnp.dot(hn, wg_bf16_ref[...], preferred_element_type=_F32)
        b = jnp.dot(hn, wu_bf16_ref[...], preferred_element_type=_F32)
        o_ref[rows, :] = ((a * jax.nn.sigmoid(a)) * b).astype(o_ref.dtype)


def _ffn_act(hn, w_gate, w_up, layer, tm=2048, tn=512, row_chunk=256):
    t, d = hn.shape
    ff = w_gate.shape[2]
    w_spec = pl.BlockSpec((None, d, tn), lambda j, i: (layer, 0, j))
    return pl.pallas_call(
        functools.partial(_ffn_act_kernel, row_chunk=row_chunk),
        grid=(ff // tn, t // tm),
        in_specs=[pl.BlockSpec((tm, d), lambda j, i: (i, 0)), w_spec, w_spec],
        out_specs=pl.BlockSpec((tm, tn), lambda j, i: (i, j)),
        out_shape=jax.ShapeDtypeStruct((t, ff), _BF16),
        scratch_shapes=[pltpu.VMEM((d, tn), _BF16), pltpu.VMEM((d, tn), _BF16)],
        compiler_params=_params(("parallel", "arbitrary")),
        name="ffn_act",
    )(hn, w_gate, w_up)


def _ffn_down_kernel(act_ref, wd_ref, h_ref, g_ref, *out_refs, row_chunk):
    xn_ref = out_refs[-1]
    for r0 in range(0, h_ref.shape[0], row_chunk):
        rows = slice(r0, r0 + row_chunk)
        x = h_ref[rows, :] + jnp.dot(act_ref[rows, :], wd_ref[...], preferred_element_type=_F32)
        if len(out_refs) == 2:
            out_refs[0][rows, :] = x
        xn_ref[rows, :] = ((x * _rms_scale(x)) * g_ref[...]).astype(xn_ref.dtype)


def _ffn_down(act, h, w_down, g_next, layer, emit_x, xn_dtype, tm=512, row_chunk=256):
    t, d = h.shape
    ff = w_down.shape[1]
    row_spec = pl.BlockSpec((tm, d), lambda i: (i, 0))
    out_specs = [row_spec, row_spec] if emit_x else [row_spec]
    out_shape = [jax.ShapeDtypeStruct((t, d), xn_dtype)]
    if emit_x:
        out_shape = [jax.ShapeDtypeStruct((t, d), _F32)] + out_shape
    return pl.pallas_call(
        functools.partial(_ffn_down_kernel, row_chunk=row_chunk),
        grid=(t // tm,),
        in_specs=[
            pl.BlockSpec((tm, ff), lambda i: (i, 0)),
            pl.BlockSpec((None, ff, d), lambda i: (layer, 0, 0), pipeline_mode=pl.Buffered(1)),
            row_spec,
            pl.BlockSpec((1, d), lambda i: (0, 0)),
        ],
        out_specs=out_specs,
        out_shape=out_shape,
        compiler_params=_params(("parallel",)),
        name="ffn_down",
    )(act, w_down, h, g_next)


def kernel(x, w_in, conv_w, w_conv_out, w_attn_out, w_mix_out, rel_bias, norm_mix, norm_ffn,
           w_ffn_gate, w_ffn_up, w_ffn_down, norm_final):
    batch, seq, d = x.shape
    depth = w_in.shape[0]
    c = conv_w.shape[2]
    aw = N_HEADS * HEAD_DIM
    t = batch * seq
    assert seq % MOBA_BLOCK == 0 and seq // MOBA_BLOCK >= MOBA_TOPK
    assert w_in.shape[2] == 3 * c + 3 * aw + 2 * d

    w_in, w_co, w_ao, w_mix = (w.astype(_BF16) for w in (w_in, w_conv_out, w_attn_out, w_mix_out))
    w_down = w_ffn_down.astype(_BF16)
    bias_own, bias_prev = _bias_tables(rel_bias)
    q_col = 3 * c

    x = x.reshape(t, d)
    for layer in range(depth):
        if layer == 0:
            proj = _in_proj(x, w_in, layer, norm_gain=norm_mix[0:1])
        else:
            proj = _in_proj(xn, w_in, layer)
        att = _moba_attention(proj, bias_own, bias_prev, batch, seq, q_col)
        h, hn = _mixer_out(proj, att, x, conv_w, w_co, w_ao, w_mix, norm_ffn[layer:layer + 1],
                           layer, seq)
        act = _ffn_act(hn, w_ffn_gate, w_ffn_up, layer)
        if layer + 1 < depth:
            x, xn = _ffn_down(act, h, w_down, norm_mix[layer + 1:layer + 2], layer,
                              emit_x=True, xn_dtype=_BF16)
        else:
            (out,) = _ffn_down(act, h, w_down, norm_final.reshape(1, d), layer,
                               emit_x=False, xn_dtype=_F32)
    return out.reshape(batch, seq, d)
```

```python
import functools
import math

import numpy as np
import jax
import jax.numpy as jnp
from jax import lax
from jax.experimental import pallas as pl
from jax.experimental.pallas import tpu as pltpu

CONV_K = 3
N_HEADS = 8
HEAD_DIM = 128
MOBA_BLOCK = 256
MOBA_TOPK = 3
NUM_BUCKETS = 32
MAX_DISTANCE = 128
EPS = 1e-6
_LOG2_E = math.log2(math.e)

_BF16 = jnp.bfloat16
_F32 = jnp.float32
_V7X_VMEM_LIMIT_BYTES = 56 * 1024 * 1024


def _params(semantics):
    return pltpu.CompilerParams(dimension_semantics=semantics,
                                vmem_limit_bytes=_V7X_VMEM_LIMIT_BYTES)


def _rms_scale(x):
    return lax.rsqrt(jnp.mean(x * x, axis=-1, keepdims=True) + EPS)


def _norm_matmul_kernel(x_ref, g_ref, w_ref, o_ref, xn_ref):
    @pl.when(pl.program_id(1) == 0)
    def _():
        x = x_ref[...]
        xn_ref[...] = ((x * _rms_scale(x)) * g_ref[...]).astype(xn_ref.dtype)

    o_ref[...] = jnp.dot(xn_ref[...], w_ref[...],
                         preferred_element_type=_F32).astype(o_ref.dtype)


def _in_proj_first(x, norm_gain, w_in_bf16, tm=1024, tn=2048):
    t, d = x.shape
    n = w_in_bf16.shape[2]
    return pl.pallas_call(
        _norm_matmul_kernel,
        grid=(t // tm, n // tn),
        in_specs=[pl.BlockSpec((tm, d), lambda i, j: (i, 0)),
                  pl.BlockSpec((1, d), lambda i, j: (0, 0)),
                  pl.BlockSpec((None, d, tn), lambda i, j: (0, 0, j))],
        out_specs=pl.BlockSpec((tm, tn), lambda i, j: (i, j)),
        out_shape=jax.ShapeDtypeStruct((t, n), _BF16),
        scratch_shapes=[pltpu.VMEM((tm, d), _BF16)],
        compiler_params=_params(("parallel", "arbitrary")),
        name="in_proj_first",
    )(x, norm_gain, w_in_bf16)


def _matmul_f32w_kernel(x_ref, w_ref, o_ref, w_bf16_ref, *, row_chunk):
    @pl.when(pl.program_id(1) == 0)
    def _():
        w_bf16_ref[...] = w_ref[...].astype(w_bf16_ref.dtype)

    for r0 in range(0, x_ref.shape[0], row_chunk):
        rows = slice(r0, r0 + row_chunk)
        o_ref[rows, :] = jnp.dot(x_ref[rows, :], w_bf16_ref[...],
                                 preferred_element_type=_F32).astype(o_ref.dtype)


def _in_proj(xn, w_in, layer, tm=2048, tn=1024, row_chunk=512):
    t, d = xn.shape
    n = w_in.shape[2]
    return pl.pallas_call(
        functools.partial(_matmul_f32w_kernel, row_chunk=row_chunk),
        grid=(n // tn, t // tm),
        in_specs=[pl.BlockSpec((tm, d), lambda j, i: (i, 0)),
                  pl.BlockSpec((None, d, tn), lambda j, i: (layer, 0, j))],
        out_specs=pl.BlockSpec((tm, tn), lambda j, i: (i, j)),
        out_shape=jax.ShapeDtypeStruct((t, n), _BF16),
        scratch_shapes=[pltpu.VMEM((d, tn), _BF16)],
        compiler_params=_params(("parallel", "arbitrary")),
        name="in_proj",
    )(xn, w_in)


def _t5_bucket_np(dist):
    n = np.maximum(dist, 0)
    max_exact = NUM_BUCKETS // 2
    nf = np.maximum(n, 1).astype(np.float64)
    large = max_exact + (np.log(nf / max_exact) / math.log(MAX_DISTANCE / max_exact)
                         * (NUM_BUCKETS - max_exact)).astype(np.int32)
    large = np.minimum(large, NUM_BUCKETS - 1)
    return np.where(n < max_exact, n, large).astype(np.int32)


def _bias_tables(rel_bias):
    blk = MOBA_BLOCK
    width = 2 * blk
    e = np.arange(width)
    d_own, d_prev = e - (blk - 1), e + 1
    idx = np.stack([_t5_bucket_np(d_own), _t5_bucket_np(d_prev)])
    table_t = rel_bias.astype(_F32).T
    heads = table_t.shape[0]
    vals = jnp.take(table_t, jnp.asarray(idx), axis=1)
    vals = vals - table_t[:, NUM_BUCKETS - 1][:, None, None]
    causal = np.stack([d_own >= 0, np.ones(width, bool)])
    vals = jnp.where(jnp.asarray(causal)[None], vals, -jnp.inf)
    skew = jnp.broadcast_to(vals[:, :, None, :], (heads, 2, blk, width))
    skew = skew.reshape(heads, 2, blk * width)[:, :, :blk * (width - 1)]
    skew = skew.reshape(heads, 2, blk, width - 1)[:, :, :, blk - 1:width - 1]
    skew = skew * _LOG2_E
    return skew[:, 0], skew[:, 1]


def _attn_kernel(q_ref, k_ref, v_ref, own_ref, prev_ref, o_ref,
                 kmean_ref, vt_ref, qbd_ref, mask_ref, m_ref, acc_ref, raw_ref, pend_p_ref, pend_a_ref,
                 *, nb, heads):
    i = pl.program_id(1)
    blk = MOBA_BLOCK
    dh = HEAD_DIM
    pairs = heads // 2
    log2_scale = dh ** -0.5 * _LOG2_E

    @pl.when(i == 0)
    def _():
        @pl.loop(0, nb)
        def _(n):
            rows = pl.ds(pl.multiple_of(n * blk, blk), blk)
            for h in range(heads):
                cols = slice(h * dh, (h + 1) * dh)
                kf = k_ref[rows, cols].astype(_F32)
                kmean_ref[h, pl.ds(n, 1), :] = jnp.sum(kf, axis=0, keepdims=True) * (1.0 / blk)
                vt_ref[h, n, 0:dh, :] = v_ref[rows, cols].T
                vt_ref[h, n, dh:, :] = jnp.ones((vt_ref.shape[2] - dh, blk), vt_ref.dtype)

    blk_id = lax.broadcasted_iota(jnp.int32, (nb, blk), 0)
    past = blk_id < i
    zeros = jnp.zeros((dh, blk), qbd_ref.dtype)
    for h in range(heads):
        g, half = divmod(h, 2)
        lanes = slice(half * blk, (half + 1) * blk)
        qf = q_ref[:, h * dh:(h + 1) * dh].T.astype(_F32)
        qs = (qf * log2_scale).astype(qbd_ref.dtype)
        qp = qf.astype(qbd_ref.dtype)
        for v, q in enumerate((qp, qs)):
            qbd_ref[v, g, half * dh:(half + 1) * dh, :] = (
                jnp.concatenate([q, zeros] if half == 0 else [zeros, q], axis=1))
        gate = jnp.dot(kmean_ref[h], qf, precision=lax.Precision.HIGHEST,
                       preferred_element_type=_F32)
        gate = jnp.where(past, gate, -jnp.inf)
        rank = jnp.zeros((nb, blk), jnp.int32)
        for n in range(nb):
            row = gate[n:n + 1, :]
            beats = (row > gate) | ((row == gate) & (n < blk_id))
            rank = rank + beats.astype(jnp.int32)
        mask_ref[g, :, lanes] = jnp.where((rank < MOBA_TOPK) & past, 0.0, -jnp.inf)

    ahead = raw_ref.shape[0]
    defer = pend_p_ref.shape[0]

    def qk(j, g, scaled):
        start = j * blk if isinstance(j, int) else pl.multiple_of(j * blk, blk)
        return jnp.dot(k_ref[pl.ds(start, blk), g * 2 * dh:(g + 1) * 2 * dh],
                       qbd_ref[int(scaled), g], preferred_element_type=_F32)

    def accumulate(g, j, p, alpha):
        for half in range(2):
            h = 2 * g + half
            lanes = slice(half * blk, (half + 1) * blk)
            pv = jnp.dot(vt_ref[h, j], p[:, lanes], preferred_element_type=_F32)
            acc_ref[h] = pv if alpha is None else alpha[:, lanes] * acc_ref[h] + pv

    def flush(pend_j, pend_first):
        for d in range(defer):
            accumulate(pairs - defer + d, pend_j, pend_p_ref[d],
                       None if pend_first else pend_a_ref[d])

    def stage(j, scores_of_pair, first, raw, next_j, pending, scaled, next_scaled):
        if pending is not None:
            flush(*pending)
        raw = [qk(j, g, scaled) for g in range(ahead)] if raw is None else list(raw)
        raw_next = []
        for g in range(pairs):
            if g + ahead < pairs:
                raw.append(qk(j, g + ahead, scaled))
            else:
                raw_next.append(qk(next_j, g + ahead - pairs, next_scaled))
            s = scores_of_pair(g, raw[g])
            m_new = jnp.max(s, axis=0, keepdims=True)
            alpha = None
            if not first:
                m_old = m_ref[g]
                m_new = jnp.maximum(m_old, m_new)
                alpha = jnp.exp2(m_old - m_new)
            m_ref[g] = m_new
            p = jnp.exp2(s - m_new).astype(vt_ref.dtype)
            if g >= pairs - defer:
                pend_p_ref[g - (pairs - defer)] = p
                if not first:
                    pend_a_ref[g - (pairs - defer)] = alpha
            else:
                accumulate(g, j, p, alpha)
        return raw_next

    jp = jnp.maximum(i - 1, 0)
    raw = stage(i, lambda g, r: r * log2_scale + own_ref[g], True, None, jp, None, False, False)
    raw = stage(jp, lambda g, r: r * log2_scale + (prev_ref[g] + mask_ref[g, pl.ds(jp, 1), :]), False,
                raw, 0, (i, True), False, True)
    for a in range(ahead):
        raw_ref[a] = raw[a]

    @pl.loop(0, jp)
    def _(j):
        raw = stage(j, lambda g, r: jnp.where(mask_ref[g, pl.ds(j, 1), :] == 0.0, r, -jnp.inf), False,
                    [raw_ref[a] for a in range(ahead)], j + 1,
                    (jnp.where(j == 0, jp, j - 1), False), True, True)
        for a in range(ahead):
            raw_ref[a] = raw[a]

    flush(jnp.maximum(jp - 1, 0), False)
    for h in range(heads):
        acc = acc_ref[h]
        out = acc[0:dh, :] * (1.0 / acc[dh:dh + 1, :])
        o_ref[:, h * dh:(h + 1) * dh] = out.T.astype(o_ref.dtype)


def _moba_attention(proj, bias_own, bias_prev, batch, seq, q_col):
    t = proj.shape[0]
    nb = seq // MOBA_BLOCK
    blk = MOBA_BLOCK
    heads = N_HEADS
    pairs = heads // 2
    w = heads * HEAD_DIM
    q_blk = q_col // w
    ones_rows = 16
    once = pl.Buffered(1)

    def pair_layout(bias):
        return bias.reshape(pairs, 2, blk, blk).transpose(0, 2, 1, 3).reshape(pairs, blk, 2 * blk)

    kernel = functools.partial(_attn_kernel, nb=nb, heads=heads)
    return pl.pallas_call(
        kernel,
        grid=(batch, nb),
        in_specs=[
            pl.BlockSpec((blk, w), lambda b, i: (b * nb + i, q_blk)),
            pl.BlockSpec((seq, w), lambda b, i: (b, q_blk + 1)),
            pl.BlockSpec((seq, w), lambda b, i: (b, q_blk + 2)),
            pl.BlockSpec((pairs, blk, 2 * blk), lambda b, i: (0, 0, 0), pipeline_mode=once),
            pl.BlockSpec((pairs, blk, 2 * blk), lambda b, i: (0, 0, 0), pipeline_mode=once),
        ],
        out_specs=pl.BlockSpec((blk, w), lambda b, i: (b * nb + i, 0)),
        out_shape=jax.ShapeDtypeStruct((t, w), _BF16),
        scratch_shapes=[
            pltpu.VMEM((heads, nb, HEAD_DIM), _F32),
            pltpu.VMEM((heads, nb, HEAD_DIM + ones_rows, blk), _BF16),
            pltpu.VMEM((2, pairs, 2 * HEAD_DIM, 2 * blk), _BF16),
            pltpu.VMEM((pairs, nb, 2 * blk), _F32),
            pltpu.VMEM((pairs, 1, 2 * blk), _F32),
            pltpu.VMEM((heads, HEAD_DIM + ones_rows, blk), _F32),
            pltpu.VMEM((pairs // 2, blk, 2 * blk), _F32),
            pltpu.VMEM((1, blk, 2 * blk), _BF16),
            pltpu.VMEM((1, 1, 2 * blk), _F32),
        ],
        compiler_params=_params(("parallel", "arbitrary")),
        name="moba_attention",
    )(proj, proj, proj, pair_layout(bias_own), pair_layout(bias_prev))


def _mixer_kernel(hin_ref, gb_ref, gc_ref, hin_halo_ref, gc_halo_ref, gconv_ref, gatt_ref,
                  att_ref, x_ref, cw_ref, wco_ref, wao_ref, wmix_ref, g_ref,
                  h_ref, hn_ref, *, tiles_per_seq, conv_chunk, row_chunk):
    i = pl.program_id(0)
    halo = hin_halo_ref.shape[0]
    keep = (i % tiles_per_seq != 0).astype(_F32)
    cw = cw_ref[...]
    row = lax.broadcasted_iota(jnp.int32, (row_chunk, conv_chunk), 0)

    for r0 in range(0, hin_ref.shape[0], row_chunk):
        rows = slice(r0, r0 + row_chunk)
        y_conv = None
        for c0 in range(0, hin_ref.shape[1], conv_chunk):
            ch = slice(c0, c0 + conv_chunk)
            u = gc_ref[rows, ch].astype(_F32) * hin_ref[rows, ch].astype(_F32)
            if r0 == 0:
                uh = gc_halo_ref[:, ch].astype(_F32) * hin_halo_ref[:, ch].astype(_F32) * keep
            else:
                before = slice(r0 - halo, r0)
                uh = gc_ref[before, ch].astype(_F32) * hin_ref[before, ch].astype(_F32)
            prev1 = uh[halo - 1:halo, :]
            prev2 = uh[halo - 2:halo - 1, :]
            u1 = jnp.where(row == 0, prev1, pltpu.roll(u, 1, 0))
            u2 = jnp.where(row == 0, prev2, jnp.where(row == 1, prev1, pltpu.roll(u, 2, 0)))
            y = cw[0:1, ch] * u2 + cw[1:2, ch] * u1 + cw[2:3, ch] * u
            yc = (gb_ref[rows, ch].astype(_F32) * y).astype(_BF16)
            part = jnp.dot(yc, wco_ref[ch, :], preferred_element_type=_F32)
            y_conv = part if y_conv is None else y_conv + part
        y_att = jnp.dot(att_ref[rows, :], wao_ref[...], preferred_element_type=_F32)
        merged = (jax.nn.sigmoid(gconv_ref[rows, :].astype(_F32)) * y_conv
                  + jax.nn.sigmoid(gatt_ref[rows, :].astype(_F32)) * y_att)
        h = x_ref[rows, :] + jnp.dot(merged.astype(_BF16), wmix_ref[...],
                                     preferred_element_type=_F32)
        h_ref[rows, :] = h
        hn_ref[rows, :] = ((h * _rms_scale(h)) * g_ref[...]).astype(hn_ref.dtype)


def _mixer_out(proj, att, x, conv_w, w_co, w_ao, w_mix, g_ffn, layer, seq, tm=512, halo=16,
               conv_chunk=256, row_chunk=256):
    t, d = x.shape
    c = conv_w.shape[2]
    aw = att.shape[1]
    assert seq % tm == 0 and tm % halo == 0
    gate_col = (3 * c + 3 * aw) // d
    resident = pl.Buffered(1)
    kernel = functools.partial(_mixer_kernel, tiles_per_seq=seq // tm, conv_chunk=conv_chunk,
                               row_chunk=row_chunk)
    halo_idx = lambda i: jnp.maximum(i * (tm // halo) - 1, 0)
    return pl.pallas_call(
        kernel,
        grid=(t // tm,),
        in_specs=[
            pl.BlockSpec((tm, c), lambda i: (i, 0)),
            pl.BlockSpec((tm, c), lambda i: (i, 1)),
            pl.BlockSpec((tm, c), lambda i: (i, 2)),
            pl.BlockSpec((halo, c), lambda i: (halo_idx(i), 0)),
            pl.BlockSpec((halo, c), lambda i: (halo_idx(i), 2)),
            pl.BlockSpec((tm, d), lambda i: (i, gate_col)),
            pl.BlockSpec((tm, d), lambda i: (i, gate_col + 1)),
            pl.BlockSpec((tm, aw), lambda i: (i, 0)),
            pl.BlockSpec((tm, d), lambda i: (i, 0)),
            pl.BlockSpec((None, CONV_K, c), lambda i: (layer, 0, 0)),
            pl.BlockSpec((None, c, d), lambda i: (layer, 0, 0), pipeline_mode=resident),
            pl.BlockSpec((None, aw, d), lambda i: (layer, 0, 0), pipeline_mode=resident),
            pl.BlockSpec((None, d, d), lambda i: (layer, 0, 0), pipeline_mode=resident),
            pl.BlockSpec((1, d), lambda i: (0, 0)),
        ],
        out_specs=[pl.BlockSpec((tm, d), lambda i: (i, 0)),
                   pl.BlockSpec((tm, d), lambda i: (i, 0))],
        out_shape=[jax.ShapeDtypeStruct((t, d), _F32),
                   jax.ShapeDtypeStruct((t, d), _BF16)],
        compiler_params=_params(("parallel",)),
        name="mixer_out",
    )(proj, proj, proj, proj, proj, proj, proj, att, x, conv_w, w_co, w_ao, w_mix, g_ffn)


def _ffn_act_kernel(hn_ref, wg_ref, wu_ref, o_ref, wg_bf16_ref, wu_bf16_ref, *, row_chunk):
    @pl.when(pl.program_id(1) == 0)
    def _():
        wg_bf16_ref[...] = wg_ref[...].astype(wg_bf16_ref.dtype)
        wu_bf16_ref[...] = wu_ref[...].astype(wu_bf16_ref.dtype)

    for r0 in range(0, hn_ref.shape[0], row_chunk):
        rows = slice(r0, r0 + row_chunk)
        hn = hn_ref[rows, :]
        a = jnp.dot(hn, wg_bf16_ref[...], preferred_element_type=_F32)
        b = jnp.dot(hn, wu_bf16_ref[...], preferred_element_type=_F32)
        o_ref[rows, :] = ((a * jax.nn.sigmoid(a)) * b).astype(o_ref.dtype)


def _ffn_act(hn, w_gate, w_up, layer, tm=2048, tn=512, row_chunk=256):
    t, d = hn.shape
    ff = w_gate.shape[2]
    w_spec = pl.BlockSpec((None, d, tn), lambda j, i: (layer, 0, j))
    return pl.pallas_call(
        functools.partial(_ffn_act_kernel, row_chunk=row_chunk),
        grid=(ff // tn, t // tm),
        in_specs=[pl.BlockSpec((tm, d), lambda j, i: (i, 0)), w_spec, w_spec],
        out_specs=pl.BlockSpec((tm, tn), lambda j, i: (i, j)),
        out_shape=jax.ShapeDtypeStruct((t, ff), _BF16),
        scratch_shapes=[pltpu.VMEM((d, tn), _BF16), pltpu.VMEM((d, tn), _BF16)],
        compiler_params=_params(("parallel", "arbitrary")),
        name="ffn_act",
    )(hn, w_gate, w_up)


def _ffn_down_kernel(act_ref, wd_ref, h_ref, g_ref, *out_refs, row_chunk):
    xn_ref = out_refs[-1]
    for r0 in range(0, h_ref.shape[0], row_chunk):
        rows = slice(r0, r0 + row_chunk)
        x = h_ref[rows, :] + jnp.dot(act_ref[rows, :], wd_ref[...], preferred_element_type=_F32)
        if len(out_refs) == 2:
            out_refs[0][rows, :] = x
        xn_ref[rows, :] = ((x * _rms_scale(x)) * g_ref[...]).astype(xn_ref.dtype)


def _ffn_down(act, h, w_down, g_next, layer, emit_x, xn_dtype, tm=512, row_chunk=256):
    t, d = h.shape
    ff = w_down.shape[1]
    row_spec = pl.BlockSpec((tm, d), lambda i: (i, 0))
    out_specs = [row_spec, row_spec] if emit_x else [row_spec]
    out_shape = [jax.ShapeDtypeStruct((t, d), xn_dtype)]
    if emit_x:
        out_shape = [jax.ShapeDtypeStruct((t, d), _F32)] + out_shape
    return pl.pallas_call(
        functools.partial(_ffn_down_kernel, row_chunk=row_chunk),
        grid=(t // tm,),
        in_specs=[
            pl.BlockSpec((tm, ff), lambda i: (i, 0)),
            pl.BlockSpec((None, ff, d), lambda i: (layer, 0, 0), pipeline_mode=pl.Buffered(1)),
            row_spec,
            pl.BlockSpec((1, d), lambda i: (0, 0)),
        ],
        out_specs=out_specs,
        out_shape=out_shape,
        compiler_params=_params(("parallel",)),
        name="ffn_down",
    )(act, w_down, h, g_next)


def kernel(x, w_in, conv_w, w_conv_out, w_attn_out, w_mix_out, rel_bias, norm_mix, norm_ffn,
           w_ffn_gate, w_ffn_up, w_ffn_down, norm_final):
    batch, seq, d = x.shape
    depth = w_in.shape[0]
    c = conv_w.shape[2]
    aw = N_HEADS * HEAD_DIM
    t = batch * seq
    assert seq % MOBA_BLOCK == 0 and seq // MOBA_BLOCK >= MOBA_TOPK
    assert w_in.shape[2] == 3 * c + 3 * aw + 2 * d

    w_in_first = w_in[0:1].astype(_BF16)
    w_co, w_ao, w_mix = (w.astype(_BF16) for w in (w_conv_out, w_attn_out, w_mix_out))
    w_down = w_ffn_down.astype(_BF16)
    bias_own, bias_prev = _bias_tables(rel_bias)
    q_col = 3 * c

    x = x.reshape(t, d)
    for layer in range(depth):
        if layer == 0:
            proj = _in_proj_first(x, norm_mix[0:1], w_in_first)
        else:
            proj = _in_proj(xn, w_in, layer)
        att = _moba_attention(proj, bias_own, bias_prev, batch, seq, q_col)
        h, hn = _mixer_out(proj, att, x, conv_w, w_co, w_ao, w_mix, norm_ffn[layer:layer + 1],
                           layer, seq)
        act = _ffn_act(hn, w_ffn_gate, w_ffn_up, layer)
        if layer + 1 < depth:
            x, xn = _ffn_down(act, h, w_down, norm_mix[layer + 1:layer + 2], layer,
                              emit_x=True, xn_dtype=_BF16)
        else:
            (out,) = _ffn_down(act, h, w_down, norm_final.reshape(1, d), layer,
                               emit_x=False, xn_dtype=_F32)
    return out.reshape(batch, seq, d)
```

```python
import functools
import math

import numpy as np
import jax
import jax.numpy as jnp
from jax import lax
from jax.experimental import pallas as pl
from jax.experimental.pallas import tpu as pltpu

CONV_K = 3
N_HEADS = 8
HEAD_DIM = 128
MOBA_BLOCK = 256
MOBA_TOPK = 3
NUM_BUCKETS = 32
MAX_DISTANCE = 128
EPS = 1e-6
_LOG2_E = math.log2(math.e)

_BF16 = jnp.bfloat16
_F32 = jnp.float32
_V7X_VMEM_LIMIT_BYTES = 56 * 1024 * 1024


def _params(semantics):
    return pltpu.CompilerParams(dimension_semantics=semantics,
                                vmem_limit_bytes=_V7X_VMEM_LIMIT_BYTES)


def _rms_scale(x):
    return lax.rsqrt(jnp.mean(x * x, axis=-1, keepdims=True) + EPS)


def _norm_matmul_kernel(x_ref, g_ref, w_ref, o_ref, xn_ref):
    @pl.when(pl.program_id(1) == 0)
    def _():
        x = x_ref[...]
        xn_ref[...] = ((x * _rms_scale(x)) * g_ref[...]).astype(xn_ref.dtype)

    o_ref[...] = jnp.dot(xn_ref[...], w_ref[...],
                         preferred_element_type=_F32).astype(o_ref.dtype)


def _in_proj_first(x, norm_gain, w_in_bf16, tm=1024, tn=2048):
    t, d = x.shape
    n = w_in_bf16.shape[2]
    return pl.pallas_call(
        _norm_matmul_kernel,
        grid=(t // tm, n // tn),
        in_specs=[pl.BlockSpec((tm, d), lambda i, j: (i, 0)),
                  pl.BlockSpec((1, d), lambda i, j: (0, 0)),
                  pl.BlockSpec((None, d, tn), lambda i, j: (0, 0, j))],
        out_specs=pl.BlockSpec((tm, tn), lambda i, j: (i, j)),
        out_shape=jax.ShapeDtypeStruct((t, n), _BF16),
        scratch_shapes=[pltpu.VMEM((tm, d), _BF16)],
        compiler_params=_params(("parallel", "arbitrary")),
        name="in_proj_first",
    )(x, norm_gain, w_in_bf16)


def _round_kernel(w_ref, o_ref):
    o_ref[...] = w_ref[...].astype(o_ref.dtype)


def _round_first_layer(w, tn=1024):
    _, d, n = w.shape
    spec = pl.BlockSpec((None, d, tn), lambda j: (0, 0, j))
    return pl.pallas_call(
        _round_kernel,
        grid=(n // tn,),
        in_specs=[spec],
        out_specs=spec,
        out_shape=jax.ShapeDtypeStruct((1, d, n), _BF16),
        compiler_params=_params(("parallel",)),
        name="round_first_layer",
    )(w)


def _matmul_f32w_kernel(x_ref, w_ref, o_ref, w_bf16_ref, *, row_chunk):
    @pl.when(pl.program_id(1) == 0)
    def _():
        w_bf16_ref[...] = w_ref[...].astype(w_bf16_ref.dtype)

    for r0 in range(0, x_ref.shape[0], row_chunk):
        rows = slice(r0, r0 + row_chunk)
        o_ref[rows, :] = jnp.dot(x_ref[rows, :], w_bf16_ref[...],
                                 preferred_element_type=_F32).astype(o_ref.dtype)


def _in_proj(xn, w_in, layer, tm=2048, tn=1024, row_chunk=512):
    t, d = xn.shape
    n = w_in.shape[2]
    return pl.pallas_call(
        functools.partial(_matmul_f32w_kernel, row_chunk=row_chunk),
        grid=(n // tn, t // tm),
        in_specs=[pl.BlockSpec((tm, d), lambda j, i: (i, 0)),
                  pl.BlockSpec((None, d, tn), lambda j, i: (layer, 0, j))],
        out_specs=pl.BlockSpec((tm, tn), lambda j, i: (i, j)),
        out_shape=jax.ShapeDtypeStruct((t, n), _BF16),
        scratch_shapes=[pltpu.VMEM((d, tn), _BF16)],
        compiler_params=_params(("parallel", "arbitrary")),
        name="in_proj",
    )(xn, w_in)


def _t5_bucket_np(dist):
    n = np.maximum(dist, 0)
    max_exact = NUM_BUCKETS // 2
    nf = np.maximum(n, 1).astype(np.float64)
    large = max_exact + (np.log(nf / max_exact) / math.log(MAX_DISTANCE / max_exact)
                         * (NUM_BUCKETS - max_exact)).astype(np.int32)
    large = np.minimum(large, NUM_BUCKETS - 1)
    return np.where(n < max_exact, n, large).astype(np.int32)


def _bias_tables(rel_bias):
    blk = MOBA_BLOCK
    width = 2 * blk
    e = np.arange(width)
    d_own, d_prev = e - (blk - 1), e + 1
    idx = np.stack([_t5_bucket_np(d_own), _t5_bucket_np(d_prev)])
    table_t = rel_bias.astype(_F32).T
    heads = table_t.shape[0]
    vals = jnp.take(table_t, jnp.asarray(idx), axis=1)
    vals = vals - table_t[:, NUM_BUCKETS - 1][:, None, None]
    causal = np.stack([d_own >= 0, np.ones(width, bool)])
    vals = jnp.where(jnp.asarray(causal)[None], vals, -jnp.inf)
    skew = jnp.broadcast_to(vals[:, :, None, :], (heads, 2, blk, width))
    skew = skew.reshape(heads, 2, blk * width)[:, :, :blk * (width - 1)]
    skew = skew.reshape(heads, 2, blk, width - 1)[:, :, :, blk - 1:width - 1]
    skew = skew * _LOG2_E
    return skew[:, 0], skew[:, 1]


def _attn_kernel(q_ref, k_ref, v_ref, own_ref, prev_ref, o_ref,
                 kmean_ref, vt_ref, qbd_ref, mask_ref, m_ref, acc_ref, raw_ref, pend_p_ref, pend_a_ref,
                 *, nb, heads):
    i = pl.program_id(1)
    blk = MOBA_BLOCK
    dh = HEAD_DIM
    pairs = heads // 2
    log2_scale = dh ** -0.5 * _LOG2_E

    @pl.when(i == 0)
    def _():
        @pl.loop(0, nb)
        def _(n):
            rows = pl.ds(pl.multiple_of(n * blk, blk), blk)
            for h in range(heads):
                cols = slice(h * dh, (h + 1) * dh)
                kf = k_ref[rows, cols].astype(_F32)
                kmean_ref[h, pl.ds(n, 1), :] = jnp.sum(kf, axis=0, keepdims=True) * (1.0 / blk)
                vt_ref[h, n, 0:dh, :] = v_ref[rows, cols].T
                vt_ref[h, n, dh:, :] = jnp.ones((vt_ref.shape[2] - dh, blk), vt_ref.dtype)

    blk_id = lax.broadcasted_iota(jnp.int32, (nb, blk), 0)
    past = blk_id < i
    zeros = jnp.zeros((dh, blk), qbd_ref.dtype)
    for h in range(heads):
        g, half = divmod(h, 2)
        lanes = slice(half * blk, (half + 1) * blk)
        qf = q_ref[:, h * dh:(h + 1) * dh].T.astype(_F32)
        qs = (qf * log2_scale).astype(qbd_ref.dtype)
        qp = qf.astype(qbd_ref.dtype)
        for v, q in enumerate((qp, qs)):
            qbd_ref[v, g, half * dh:(half + 1) * dh, :] = (
                jnp.concatenate([q, zeros] if half == 0 else [zeros, q], axis=1))
        gate = jnp.dot(kmean_ref[h], qf, precision=lax.Precision.HIGHEST,
                       preferred_element_type=_F32)
        gate = jnp.where(past, gate, -jnp.inf)
        rank = jnp.zeros((nb, blk), jnp.int32)
        for n in range(nb):
            row = gate[n:n + 1, :]
            beats = (row > gate) | ((row == gate) & (n < blk_id))
            rank = rank + beats.astype(jnp.int32)
        mask_ref[g, :, lanes] = jnp.where((rank < MOBA_TOPK) & past, 0.0, -jnp.inf)

    ahead = raw_ref.shape[0]
    defer = pend_p_ref.shape[0]

    def qk(j, g, scaled):
        start = j * blk if isinstance(j, int) else pl.multiple_of(j * blk, blk)
        return jnp.dot(k_ref[pl.ds(start, blk), g * 2 * dh:(g + 1) * 2 * dh],
                       qbd_ref[int(scaled), g], preferred_element_type=_F32)

    def accumulate(g, j, p, alpha):
        for half in range(2):
            h = 2 * g + half
            lanes = slice(half * blk, (half + 1) * blk)
            pv = jnp.dot(vt_ref[h, j], p[:, lanes], preferred_element_type=_F32)
            acc_ref[h] = pv if alpha is None else alpha[:, lanes] * acc_ref[h] + pv

    def flush(pend_j, pend_first):
        for d in range(defer):
            accumulate(pairs - defer + d, pend_j, pend_p_ref[d],
                       None if pend_first else pend_a_ref[d])

    def stage(j, scores_of_pair, first, raw, next_j, pending, scaled, next_scaled):
        if pending is not None:
            flush(*pending)
        raw = [qk(j, g, scaled) for g in range(ahead)] if raw is None else list(raw)
        raw_next = []
        for g in range(pairs):
            if g + ahead < pairs:
                raw.append(qk(j, g + ahead, scaled))
            else:
                raw_next.append(qk(next_j, g + ahead - pairs, next_scaled))
            s = scores_of_pair(g, raw[g])
            m_new = jnp.max(s, axis=0, keepdims=True)
            alpha = None
            if not first:
                m_old = m_ref[g]
                m_new = jnp.maximum(m_old, m_new)
                alpha = jnp.exp2(m_old - m_new)
            m_ref[g] = m_new
            p = jnp.exp2(s - m_new).astype(vt_ref.dtype)
            if g >= pairs - defer:
                pend_p_ref[g - (pairs - defer)] = p
                if not first:
                    pend_a_ref[g - (pairs - defer)] = alpha
            else:
                accumulate(g, j, p, alpha)
        return raw_next

    jp = jnp.maximum(i - 1, 0)
    raw = stage(i, lambda g, r: r * log2_scale + own_ref[g], True, None, jp, None, False, False)
    raw = stage(jp, lambda g, r: r * log2_scale + (prev_ref[g] + mask_ref[g, pl.ds(jp, 1), :]), False,
                raw, 0, (i, True), False, True)
    for a in range(ahead):
        raw_ref[a] = raw[a]

    @pl.loop(0, jp)
    def _(j):
        raw = stage(j, lambda g, r: jnp.where(mask_ref[g, pl.ds(j, 1), :] == 0.0, r, -jnp.inf), False,
                    [raw_ref[a] for a in range(ahead)], j + 1,
                    (jnp.where(j == 0, jp, j - 1), False), True, True)
        for a in range(ahead):
            raw_ref[a] = raw[a]

    flush(jnp.maximum(jp - 1, 0), False)
    for h in range(heads):
        acc = acc_ref[h]
        out = acc[0:dh, :] * (1.0 / acc[dh:dh + 1, :])
        o_ref[:, h * dh:(h + 1) * dh] = out.T.astype(o_ref.dtype)


def _moba_attention(proj, bias_own, bias_prev, batch, seq, q_col):
    t = proj.shape[0]
    nb = seq // MOBA_BLOCK
    blk = MOBA_BLOCK
    heads = N_HEADS
    pairs = heads // 2
    w = heads * HEAD_DIM
    q_blk = q_col // w
    ones_rows = 16
    once = pl.Buffered(1)

    def pair_layout(bias):
        return bias.reshape(pairs, 2, blk, blk).transpose(0, 2, 1, 3).reshape(pairs, blk, 2 * blk)

    kernel = functools.partial(_attn_kernel, nb=nb, heads=heads)
    return pl.pallas_call(
        kernel,
        grid=(batch, nb),
        in_specs=[
            pl.BlockSpec((blk, w), lambda b, i: (b * nb + i, q_blk)),
            pl.BlockSpec((seq, w), lambda b, i: (b, q_blk + 1)),
            pl.BlockSpec((seq, w), lambda b, i: (b, q_blk + 2)),
            pl.BlockSpec((pairs, blk, 2 * blk), lambda b, i: (0, 0, 0), pipeline_mode=once),
            pl.BlockSpec((pairs, blk, 2 * blk), lambda b, i: (0, 0, 0), pipeline_mode=once),
        ],
        out_specs=pl.BlockSpec((blk, w), lambda b, i: (b * nb + i, 0)),
        out_shape=jax.ShapeDtypeStruct((t, w), _BF16),
        scratch_shapes=[
            pltpu.VMEM((heads, nb, HEAD_DIM), _F32),
            pltpu.VMEM((heads, nb, HEAD_DIM + ones_rows, blk), _BF16),
            pltpu.VMEM((2, pairs, 2 * HEAD_DIM, 2 * blk), _BF16),
            pltpu.VMEM((pairs, nb, 2 * blk), _F32),
            pltpu.VMEM((pairs, 1, 2 * blk), _F32),
            pltpu.VMEM((heads, HEAD_DIM + ones_rows, blk), _F32),
            pltpu.VMEM((pairs // 2, blk, 2 * blk), _F32),
            pltpu.VMEM((1, blk, 2 * blk), _BF16),
            pltpu.VMEM((1, 1, 2 * blk), _F32),
        ],
        compiler_params=_params(("parallel", "arbitrary")),
        name="moba_attention",
    )(proj, proj, proj, pair_layout(bias_own), pair_layout(bias_prev))


def _mixer_kernel(hin_ref, gb_ref, gc_ref, hin_halo_ref, gc_halo_ref, gconv_ref, gatt_ref,
                  att_ref, x_ref, cw_ref, wco_ref, wao_ref, wmix_ref, g_ref,
                  h_ref, hn_ref, *, tiles_per_seq, conv_chunk, row_chunk):
    i = pl.program_id(0)
    halo = hin_halo_ref.shape[0]
    keep = (i % tiles_per_seq != 0).astype(_F32)
    cw = cw_ref[...]
    row = lax.broadcasted_iota(jnp.int32, (row_chunk, conv_chunk), 0)

    for r0 in range(0, hin_ref.shape[0], row_chunk):
        rows = slice(r0, r0 + row_chunk)
        y_conv = None
        for c0 in range(0, hin_ref.shape[1], conv_chunk):
            ch = slice(c0, c0 + conv_chunk)
            u = gc_ref[rows, ch].astype(_F32) * hin_ref[rows, ch].astype(_F32)
            if r0 == 0:
                uh = gc_halo_ref[:, ch].astype(_F32) * hin_halo_ref[:, ch].astype(_F32) * keep
            else:
                before = slice(r0 - halo, r0)
                uh = gc_ref[before, ch].astype(_F32) * hin_ref[before, ch].astype(_F32)
            prev1 = uh[halo - 1:halo, :]
            prev2 = uh[halo - 2:halo - 1, :]
            u1 = jnp.where(row == 0, prev1, pltpu.roll(u, 1, 0))
            u2 = jnp.where(row == 0, prev2, jnp.where(row == 1, prev1, pltpu.roll(u, 2, 0)))
            y = cw[0:1, ch] * u2 + cw[1:2, ch] * u1 + cw[2:3, ch] * u
            yc = (gb_ref[rows, ch].astype(_F32) * y).astype(_BF16)
            part = jnp.dot(yc, wco_ref[ch, :], preferred_element_type=_F32)
            y_conv = part if y_conv is None else y_conv + part
        y_att = jnp.dot(att_ref[rows, :], wao_ref[...], preferred_element_type=_F32)
        merged = (jax.nn.sigmoid(gconv_ref[rows, :].astype(_F32)) * y_conv
                  + jax.nn.sigmoid(gatt_ref[rows, :].astype(_F32)) * y_att)
        h = x_ref[rows, :] + jnp.dot(merged.astype(_BF16), wmix_ref[...],
                                     preferred_element_type=_F32)
        h_ref[rows, :] = h
        hn_ref[rows, :] = ((h * _rms_scale(h)) * g_ref[...]).astype(hn_ref.dtype)


def _mixer_out(proj, att, x, conv_w, w_co, w_ao, w_mix, g_ffn, layer, seq, tm=512, halo=16,
               conv_chunk=256, row_chunk=256):
    t, d = x.shape
    c = conv_w.shape[2]
    aw = att.shape[1]
    assert seq % tm == 0 and tm % halo == 0
    gate_col = (3 * c + 3 * aw) // d
    resident = pl.Buffered(1)
    kernel = functools.partial(_mixer_kernel, tiles_per_seq=seq // tm, conv_chunk=conv_chunk,
                               row_chunk=row_chunk)
    halo_idx = lambda i: jnp.maximum(i * (tm // halo) - 1, 0)
    return pl.pallas_call(
        kernel,
        grid=(t // tm,),
        in_specs=[
            pl.BlockSpec((tm, c), lambda i: (i, 0)),
            pl.BlockSpec((tm, c), lambda i: (i, 1)),
            pl.BlockSpec((tm, c), lambda i: (i, 2)),
            pl.BlockSpec((halo, c), lambda i: (halo_idx(i), 0)),
            pl.BlockSpec((halo, c), lambda i: (halo_idx(i), 2)),
            pl.BlockSpec((tm, d), lambda i: (i, gate_col)),
            pl.BlockSpec((tm, d), lambda i: (i, gate_col + 1)),
            pl.BlockSpec((tm, aw), lambda i: (i, 0)),
            pl.BlockSpec((tm, d), lambda i: (i, 0)),
            pl.BlockSpec((None, CONV_K, c), lambda i: (layer, 0, 0)),
            pl.BlockSpec((None, c, d), lambda i: (layer, 0, 0), pipeline_mode=resident),
            pl.BlockSpec((None, aw, d), lambda i: (layer, 0, 0), pipeline_mode=resident),
            pl.BlockSpec((None, d, d), lambda i: (layer, 0, 0), pipeline_mode=resident),
            pl.BlockSpec((1, d), lambda i: (0, 0)),
        ],
        out_specs=[pl.BlockSpec((tm, d), lambda i: (i, 0)),
                   pl.BlockSpec((tm, d), lambda i: (i, 0))],
        out_shape=[jax.ShapeDtypeStruct((t, d), _F32),
                   jax.ShapeDtypeStruct((t, d), _BF16)],
        compiler_params=_params(("parallel",)),
        name="mixer_out",
    )(proj, proj, proj, proj, proj, proj, proj, att, x, conv_w, w_co, w_ao, w_mix, g_ffn)


def _ffn_act_kernel(hn_ref, wg_ref, wu_ref, o_ref, wg_bf16_ref, wu_bf16_ref, *, row_chunk):
    @pl.when(pl.program_id(1) == 0)
    def _():
        wg_bf16_ref[...] = wg_ref[...].astype(wg_bf16_ref.dtype)
        wu_bf16_ref[...] = wu_ref[...].astype(wu_bf16_ref.dtype)

    for r0 in range(0, hn_ref.shape[0], row_chunk):
        rows = slice(r0, r0 + row_chunk)
        hn = hn_ref[rows, :]
        a = jnp.dot(hn, wg_bf16_ref[...], preferred_element_type=_F32)
        b = jnp.dot(hn, wu_bf16_ref[...], preferred_element_type=_F32)
        o_ref[rows, :] = ((a * jax.nn.sigmoid(a)) * b).astype(o_ref.dtype)


def _ffn_act(hn, w_gate, w_up, layer, tm=2048, tn=512, row_chunk=256):
    t, d = hn.shape
    ff = w_gate.shape[2]
    w_spec = pl.BlockSpec((None, d, tn), lambda j, i: (layer, 0, j))
    return pl.pallas_call(
        functools.partial(_ffn_act_kernel, row_chunk=row_chunk),
        grid=(ff // tn, t // tm),
        in_specs=[pl.BlockSpec((tm, d), lambda j, i: (i, 0)), w_spec, w_spec],
        out_specs=pl.BlockSpec((tm, tn), lambda j, i: (i, j)),
        out_shape=jax.ShapeDtypeStruct((t, ff), _BF16),
        scratch_shapes=[pltpu.VMEM((d, tn), _BF16), pltpu.VMEM((d, tn), _BF16)],
        compiler_params=_params(("parallel", "arbitrary")),
        name="ffn_act",
    )(hn, w_gate, w_up)


def _ffn_down_kernel(act_ref, wd_ref, h_ref, g_ref, *out_refs, row_chunk):
    xn_ref = out_refs[-1]
    for r0 in range(0, h_ref.shape[0], row_chunk):
        rows = slice(r0, r0 + row_chunk)
        x = h_ref[rows, :] + jnp.dot(act_ref[rows, :], wd_ref[...], preferred_element_type=_F32)
        if len(out_refs) == 2:
            out_refs[0][rows, :] = x
        xn_ref[rows, :] = ((x * _rms_scale(x)) * g_ref[...]).astype(xn_ref.dtype)


def _ffn_down(act, h, w_down, g_next, layer, emit_x, xn_dtype, tm=512, row_chunk=256):
    t, d = h.shape
    ff = w_down.shape[1]
    row_spec = pl.BlockSpec((tm, d), lambda i: (i, 0))
    out_specs = [row_spec, row_spec] if emit_x else [row_spec]
    out_shape = [jax.ShapeDtypeStruct((t, d), xn_dtype)]
    if emit_x:
        out_shape = [jax.ShapeDtypeStruct((t, d), _F32)] + out_shape
    return pl.pallas_call(
        functools.partial(_ffn_down_kernel, row_chunk=row_chunk),
        grid=(t // tm,),
        in_specs=[
            pl.BlockSpec((tm, ff), lambda i: (i, 0)),
            pl.BlockSpec((None, ff, d), lambda i: (layer, 0, 0), pipeline_mode=pl.Buffered(1)),
            row_spec,
            pl.BlockSpec((1, d), lambda i: (0, 0)),
        ],
        out_specs=out_specs,
        out_shape=out_shape,
        compiler_params=_params(("parallel",)),
        name="ffn_down",
    )(act, w_down, h, g_next)


def kernel(x, w_in, conv_w, w_conv_out, w_attn_out, w_mix_out, rel_bias, norm_mix, norm_ffn,
           w_ffn_gate, w_ffn_up, w_ffn_down, norm_final):
    batch, seq, d = x.shape
    depth = w_in.shape[0]
    c = conv_w.shape[2]
    aw = N_HEADS * HEAD_DIM
    t = batch * seq
    assert seq % MOBA_BLOCK == 0 and seq // MOBA_BLOCK >= MOBA_TOPK
    assert w_in.shape[2] == 3 * c + 3 * aw + 2 * d

    w_in_first = _round_first_layer(w_in)
    w_co, w_ao, w_mix = (w.astype(_BF16) for w in (w_conv_out, w_attn_out, w_mix_out))
    w_down = w_ffn_down.astype(_BF16)
    bias_own, bias_prev = _bias_tables(rel_bias)
    q_col = 3 * c

    x = x.reshape(t, d)
    for layer in range(depth):
        if layer == 0:
            proj = _in_proj_first(x, norm_mix[0:1], w_in_first)
        else:
            proj = _in_proj(xn, w_in, layer)
        att = _moba_attention(proj, bias_own, bias_prev, batch, seq, q_col)
        h, hn = _mixer_out(proj, att, x, conv_w, w_co, w_ao, w_mix, norm_ffn[layer:layer + 1],
                           layer, seq)
        act = _ffn_act(hn, w_ffn_gate, w_ffn_up, layer)
        if layer + 1 < depth:
            x, xn = _ffn_down(act, h, w_down, norm_mix[layer + 1:layer + 2], layer,
                              emit_x=True, xn_dtype=_BF16)
        else:
            (out,) = _ffn_down(act, h, w_down, norm_final.reshape(1, d), layer,
                               emit_x=False, xn_dtype=_F32)
    return out.reshape(batch, seq, d)
```
